```python
import functools
import jax
import jax.numpy as jnp
from jax import lax
import numpy as np

D_MODEL = 2048
BATCH = 2
SEQ = 16384
DEPTH = 2

CTX_LEN = 256
GRID_W = 64

GLA_HEADS = 4
GLA_DK = 128
GLA_DV = 256
GLA_GATE_RANK = 16
GLA_TAU = 16.0
GLA_CHUNK = 64
LRU_WIDTH = 1024
LRU_BLOCKS = 8
LRU_C = 8.0
CONV_W = 4
MLSTM_HEADS = 8
MLSTM_DK = 128
MLSTM_DV = 256
MLSTM_CHUNK = 64
M_INIT = -1e30
N_GROUPS = 4
EXPERTS_PER_GROUP = 8
N_EXPERTS = N_GROUPS * EXPERTS_PER_GROUP
TOP_K = 2
D_EXPERT = 512
MOE_BLOCK = 128
DN_ALPHA = (2 * DEPTH) ** 0.25
DN_BETA = (8 * DEPTH) ** -0.25
LN_EPS = 1e-5
N_EVEN = (DEPTH + 1) // 2
N_ODD = DEPTH // 2
GLA_QK = GLA_HEADS * GLA_DK
GLA_V = GLA_HEADS * GLA_DV
EVEN_SIZES = (GLA_QK, GLA_QK, GLA_V, GLA_V, GLA_GATE_RANK, GLA_GATE_RANK, LRU_WIDTH, LRU_WIDTH)
EVEN_IN = sum(EVEN_SIZES)
EVEN_MIX = GLA_V + LRU_WIDTH
ML_QK = MLSTM_HEADS * MLSTM_DK
ML_V = MLSTM_HEADS * MLSTM_DV
ODD_SIZES = (ML_QK, ML_QK, ML_V, ML_V, MLSTM_HEADS, MLSTM_HEADS, MLSTM_HEADS, MLSTM_HEADS)
ODD_IN = sum(ODD_SIZES)

kernel_name = 'hybrid_gla_rglru_mlstm_hmoe_dit'


def layer_norm(x, g, b):
    xf = x.astype(jnp.float32)
    mu = jnp.mean(xf, -1, keepdims=True)
    var = jnp.mean(jnp.square(xf - mu), -1, keepdims=True)
    return ((xf - mu) * lax.rsqrt(var + LN_EPS)).astype(x.dtype) * g + b


def head_rms_norm(h, g):
    return h * lax.rsqrt(jnp.mean(jnp.square(h), -1, keepdims=True) + LN_EPS) * g


def cond_modulation(cond, w, b):
    return (jax.nn.silu(cond) @ w + b).reshape(cond.shape[0], 6, -1)


def modulate(h, m, j):
    return h * (1 + m[:, j + 1, None]) + m[:, j, None]


def split_cols(p, sizes):
    out, off = [], 0
    for s in sizes:
        out.append(p[..., off:off + s])
        off += s
    return out


def split_heads(a, n_heads):
    B, T, _ = a.shape
    return a.reshape(B, T, n_heads, -1).transpose(0, 2, 1, 3)


def merge_heads(a):
    B, H, T, d = a.shape
    return a.transpose(0, 2, 1, 3).reshape(B, T, H * d)


def to_chunks(a, chunk):
    B, H, T = a.shape[:3]
    return jnp.moveaxis(a.reshape((B, H, T // chunk, chunk) + a.shape[3:]), 2, 0)


def from_chunks(a):
    a = jnp.moveaxis(a, 0, 2)
    return a.reshape(a.shape[:2] + (-1,) + a.shape[4:])


def centred_dwconv(x, w, b):
    left = CONV_W // 2
    right = CONV_W - 1 - left
    T = x.shape[1]
    xp = jnp.pad(x, ((0, 0), (left, right), (0, 0)))
    return b + sum(xp[:, j:j + T] * w[j] for j in range(CONV_W))


def bidir_with_prefix(scan_f, scan_b, ctx_f, lat_f, ctx_b, lat_b, init, t_axis):
    def flip(a):
        return jnp.flip(a, t_axis)
    yc_f, st_f = scan_f(*ctx_f, init)
    yl_f, _ = scan_f(*lat_f, st_f)
    yc_b, st_b = scan_b(*[flip(a) for a in ctx_b], init)
    yl_b, _ = scan_b(*[flip(a) for a in lat_b], st_b)
    return yc_f + flip(yc_b), yl_f + flip(yl_b)


def gla_chunk_scan(q, k, v, log_a, s0):
    mask = jnp.tril(jnp.ones((GLA_CHUNK, GLA_CHUNK), bool))

    def step(s, blk):
        qc, kc, vc, gc = blk
        b = jnp.cumsum(gc, axis=-2)
        b_last = b[..., -1:, :]
        q_dec = qc * jnp.exp(b)
        k_inv = kc * jnp.exp(-b)
        k_end = kc * jnp.exp(b_last - b)
        att = jnp.where(mask, jnp.einsum('bhtd,bhsd->bhts', q_dec, k_inv), 0.0)
        o = jnp.einsum('bhts,bhsv->bhtv', att, vc) + jnp.einsum('bhtd,bhdv->bhtv', q_dec, s)
        s_new = jnp.swapaxes(jnp.exp(b_last), -1, -2) * s + jnp.einsum('bhsd,bhsv->bhdv', k_end, vc)
        return s_new, o

    xs = tuple(to_chunks(a, GLA_CHUNK) for a in (q, k, v, log_a))
    s_fin, o = lax.scan(step, s0, xs)
    return from_chunks(o), s_fin


def rglru_scan(xc, h0, *, w_r, b_r, w_i, b_i, lam):
    B, T, W = xc.shape
    xb = xc.reshape(B, T, LRU_BLOCKS, W // LRU_BLOCKS)
    r = jax.nn.sigmoid(jnp.einsum('btnc,ncd->btnd', xb, w_r).reshape(B, T, W) + b_r)
    i = jax.nn.sigmoid(jnp.einsum('btnc,ncd->btnd', xb, w_i).reshape(B, T, W) + b_i)
    log_a = LRU_C * r * jax.nn.log_sigmoid(lam)
    a = jnp.exp(log_a)
    u = jnp.sqrt(-jnp.expm1(2.0 * log_a)) * (i * xc)
    u = u.at[:, 0].add(a[:, 0] * h0)

    def combine(left, right):
        a_l, u_l = left
        a_r, u_r = right
        return a_l * a_r, a_r * u_l + u_r

    _, h = lax.associative_scan(combine, (a, u), axis=1)
    return h, h[:, -1]


def mlstm_chunk_scan(q, k, v, i_pre, f_log, state):
    mask = jnp.tril(jnp.ones((MLSTM_CHUNK, MLSTM_CHUNK), bool))

    def step(carry, blk):
        cs, ns, m = carry
        qc, kc, vc, ic, fc = blk
        b = jnp.cumsum(fc, axis=-1)
        dmat = jnp.where(mask, b[..., :, None] - b[..., None, :] + ic[..., None, :], -jnp.inf)
        inter = b + m[..., None]
        m_t = jnp.maximum(inter, jnp.max(dmat, -1))
        w_intra = jnp.exp(dmat - m_t[..., None])
        w_inter = jnp.exp(inter - m_t)
        s = jnp.einsum('bhtd,bhsd->bhts', qc, kc) * w_intra
        num = jnp.einsum('bhts,bhsv->bhtv', s, vc) + w_inter[..., None] * jnp.einsum('bhtd,bhdv->bhtv', qc, cs)
        den = jnp.sum(s, -1) + w_inter * jnp.einsum('bhtd,bhd->bht', qc, ns)
        h = num / jnp.maximum(jnp.abs(den), jnp.exp(-m_t))[..., None]
        b_last = b[..., -1]
        g = b_last[..., None] - b + ic
        m_new = jnp.maximum(b_last + m, jnp.max(g, -1))
        wk = jnp.exp(g - m_new[..., None])
        decay = jnp.exp(b_last + m - m_new)
        cs_new = decay[..., None, None] * cs + jnp.einsum('bhs,bhsd,bhsv->bhdv', wk, kc, vc)
        ns_new = decay[..., None] * ns + jnp.einsum('bhs,bhsd->bhd', wk, kc)
        return (cs_new, ns_new, m_new), h

    xs = tuple(to_chunks(a, MLSTM_CHUNK) for a in (q, k, v, i_pre, f_log))
    st, h = lax.scan(step, state, xs)
    return from_chunks(h), st


def even_mixer(u_lat, u_ctx, w_in, w_a2, b_a, gla_g, conv_w, conv_b, w_r, b_r, w_i, b_i, lam, w_out, ctx_out):
    def prep(u):
        q, k, v, r, af, ab, xr, xg = split_cols((u @ w_in).astype(jnp.float32), EVEN_SIZES)
        log_af = split_heads(jax.nn.log_sigmoid(af @ w_a2[0] + b_a[0]) / GLA_TAU, GLA_HEADS)
        log_ab = split_heads(jax.nn.log_sigmoid(ab @ w_a2[1] + b_a[1]) / GLA_TAU, GLA_HEADS)
        q = split_heads(q, GLA_HEADS) * GLA_DK ** -0.5
        k = split_heads(k, GLA_HEADS)
        v = split_heads(v, GLA_HEADS)
        xc = centred_dwconv(xr, conv_w, conv_b)
        return (q, k, v, log_af), (q, k, v, log_ab), r, (xc,), xg

    cf, cb, rc, xcc, gc = prep(u_ctx)
    lf, lb, rl, xcl, gl = prep(u_lat)
    B = u_lat.shape[0]
    s0 = jnp.zeros((B, GLA_HEADS, GLA_DK, GLA_DV), jnp.float32)
    o_ctx, o_lat = bidir_with_prefix(gla_chunk_scan, gla_chunk_scan, cf, lf, cb, lb, s0, 2)
    lru_f = functools.partial(rglru_scan, w_r=w_r[0], b_r=b_r[0], w_i=w_i[0], b_i=b_i[0], lam=lam[0])
    lru_b = functools.partial(rglru_scan, w_r=w_r[1], b_r=b_r[1], w_i=w_i[1], b_i=b_i[1], lam=lam[1])
    h0 = jnp.zeros((B, LRU_WIDTH), jnp.float32)
    r_ctx, r_lat = bidir_with_prefix(lru_f, lru_b, xcc, xcl, xcc, xcl, h0, 1)

    def finish(o, r, h, g, dtype):
        o = merge_heads(head_rms_norm(o, gla_g)) * jax.nn.silu(r)
        y = h * jax.nn.gelu(g)
        return jnp.concatenate([o, y], -1).astype(dtype) @ w_out

    y_lat = finish(o_lat, rl, r_lat, gl, u_lat.dtype)
    y_ctx = finish(o_ctx, rc, r_ctx, gc, u_ctx.dtype) if ctx_out else None
    return y_lat, y_ctx


def odd_mixer(u_lat, u_ctx, w_in, b_gate, ml_g, w_out, ctx_out):
    B, S, D = u_lat.shape
    rows = S // GRID_W
    u_col = u_lat.reshape(B, rows, GRID_W, D).transpose(0, 2, 1, 3).reshape(B, S, D)

    def gate(a, j):
        return jnp.swapaxes(a, 1, 2) + b_gate[j][None, :, None]

    def prep(u):
        q, k, v, o, i_f, f_f, i_b, f_b = split_cols((u @ w_in).astype(jnp.float32), ODD_SIZES)
        q = split_heads(q, MLSTM_HEADS)
        k = split_heads(k, MLSTM_HEADS) * MLSTM_DK ** -0.5
        v = split_heads(v, MLSTM_HEADS)
        fwd = (q, k, v, gate(i_f, 0), jax.nn.log_sigmoid(gate(f_f, 1)))
        bwd = (q, k, v, gate(i_b, 2), jax.nn.log_sigmoid(gate(f_b, 3)))
        return fwd, bwd, o

    cf, cb, oc = prep(u_ctx)
    lf, lb, ol = prep(u_col)
    init = (jnp.zeros((B, MLSTM_HEADS, MLSTM_DK, MLSTM_DV), jnp.float32),
            jnp.zeros((B, MLSTM_HEADS, MLSTM_DK), jnp.float32),
            jnp.full((B, MLSTM_HEADS), M_INIT, jnp.float32))
    h_ctx, h_lat = bidir_with_prefix(mlstm_chunk_scan, mlstm_chunk_scan, cf, lf, cb, lb, init, 2)

    def finish(h, o, dtype):
        return (merge_heads(head_rms_norm(h, ml_g)) * jax.nn.sigmoid(o)).astype(dtype) @ w_out

    y_col = finish(h_lat, ol, u_lat.dtype)
    y_lat = y_col.reshape(B, GRID_W, rows, D).transpose(0, 2, 1, 3).reshape(B, S, D)
    y_ctx = finish(h_ctx, oc, u_ctx.dtype) if ctx_out else None
    return y_lat, y_ctx


def hier_moe(u, w_group, b_group, w_expert, b_expert, w_gate, w_up, w_down):
    T, D = u.shape
    uf = u.astype(jnp.float32)
    g_logits = uf @ w_group + b_group
    g_idx = jnp.argmax(g_logits, -1)
    p_g = jnp.take_along_axis(jax.nn.softmax(g_logits, -1), g_idx[:, None], 1)
    e_logits = (uf @ w_expert + b_expert).reshape(T, N_GROUPS, EXPERTS_PER_GROUP)
    e_logits = jnp.take_along_axis(e_logits, g_idx[:, None, None], 1)[:, 0]
    top_p, top_i = lax.top_k(jax.nn.softmax(e_logits, -1), TOP_K)
    gate = (p_g * top_p / jnp.sum(top_p, -1, keepdims=True)).reshape(-1)
    expert_id = (g_idx[:, None] * EXPERTS_PER_GROUP + top_i).reshape(-1).astype(jnp.int32)
    token_id = jnp.repeat(jnp.arange(T, dtype=jnp.int32), TOP_K)
    n_assign = T * TOP_K
    counts = jnp.zeros((N_EXPERTS,), jnp.int32).at[expert_id].add(1)
    padded = (counts + MOE_BLOCK - 1) // MOE_BLOCK * MOE_BLOCK
    pad_end = jnp.cumsum(padded)
    pad_start = pad_end - padded
    cnt_start = jnp.cumsum(counts) - counts
    order = jnp.argsort(expert_id)
    e_sorted = expert_id[order]
    dest = pad_start[e_sorted] + jnp.arange(n_assign, dtype=jnp.int32) - cnt_start[e_sorted]
    n_blocks = -(-(n_assign + N_EXPERTS * (MOE_BLOCK - 1)) // MOE_BLOCK)
    n_slots = n_blocks * MOE_BLOCK
    slot_tok = jnp.full((n_slots,), T, jnp.int32).at[dest].set(token_id[order])
    slot_gate = jnp.zeros((n_slots,), jnp.float32).at[dest].set(gate[order])
    block_start = jnp.arange(n_blocks, dtype=jnp.int32) * MOE_BLOCK
    block_e = jnp.minimum(jnp.searchsorted(pad_end, block_start, side='right'), N_EXPERTS - 1)
    u_pad = jnp.concatenate([u, jnp.zeros((1, D), u.dtype)], 0)

    def expert_block(args):
        tok, g, e = args
        xb = u_pad[tok]
        h = jax.nn.silu(xb @ w_gate[e]) * (xb @ w_up[e])
        return (h @ w_down[e]) * g[:, None].astype(u.dtype)

    y = lax.map(expert_block, (slot_tok.reshape(n_blocks, MOE_BLOCK), slot_gate.reshape(n_blocks, MOE_BLOCK), block_e))
    out = jnp.zeros((T + 1, D), y.dtype).at[slot_tok].add(y.reshape(n_slots, D))
    return out[:T]


def setup_inputs(seed: int = 0) -> dict:
    key = jax.random.key(seed)
    keys = list(jax.random.split(key, 40))

    def nrm(shape, std):
        return std * jax.random.normal(keys.pop(), shape, jnp.float32)

    D = D_MODEL
    bw = LRU_WIDTH // LRU_BLOCKS
    x = nrm((BATCH, SEQ, D), 1.0)
    c = nrm((BATCH, D), 1.0)
    ctx = nrm((BATCH, CTX_LEN, D), 1.0)
    c_ctx = nrm((D,), 1.0)
    w_mod = nrm((DEPTH, D, 6 * D), 0.5 * D ** -0.5)
    b_mod = nrm((DEPTH, 6 * D), 0.02)
    ln_g = 1.0 + nrm((DEPTH, 2, D), 0.02)
    ln_b = nrm((DEPTH, 2, D), 0.02)
    ev_w_in = nrm((N_EVEN, D, EVEN_IN), D ** -0.5)
    gla_w_a2 = nrm((N_EVEN, 2, GLA_GATE_RANK, GLA_QK), GLA_GATE_RANK ** -0.5)
    gla_b_a = nrm((N_EVEN, 2, GLA_QK), 0.1)
    gla_norm = 1.0 + nrm((N_EVEN, GLA_DV), 0.02)
    lru_conv_w = nrm((N_EVEN, CONV_W, LRU_WIDTH), CONV_W ** -0.5)
    lru_conv_b = nrm((N_EVEN, LRU_WIDTH), 0.02)
    lru_w_r = nrm((N_EVEN, 2, LRU_BLOCKS, bw, bw), bw ** -0.5)
    lru_b_r = nrm((N_EVEN, 2, LRU_WIDTH), 0.1)
    lru_w_i = nrm((N_EVEN, 2, LRU_BLOCKS, bw, bw), bw ** -0.5)
    lru_b_i = nrm((N_EVEN, 2, LRU_WIDTH), 0.1)
    a_pow = jax.random.uniform(keys.pop(), (N_EVEN, 2, LRU_WIDTH), jnp.float32, 0.9, 0.999) ** (1.0 / LRU_C)
    lru_lam = jnp.log(a_pow) - jnp.log1p(-a_pow)
    ev_w_out = nrm((N_EVEN, EVEN_MIX, D), DN_BETA * EVEN_MIX ** -0.5)
    od_w_in = nrm((N_ODD, D, ODD_IN), D ** -0.5)
    f_bias = jnp.linspace(3.0, 6.0, MLSTM_HEADS, dtype=jnp.float32)
    zeros_h = jnp.zeros((MLSTM_HEADS,), jnp.float32)
    mlstm_b_gate = jnp.stack([zeros_h, f_bias, zeros_h, f_bias])[None] + nrm((N_ODD, 4, MLSTM_HEADS), 0.1)
    mlstm_norm = 1.0 + nrm((N_ODD, MLSTM_DV), 0.02)
    od_w_out = nrm((N_ODD, ML_V, D), DN_BETA * ML_V ** -0.5)
    moe_w_group = nrm((DEPTH, D, N_GROUPS), D ** -0.5)
    moe_b_group = nrm((DEPTH, N_GROUPS), 0.01)
    moe_w_expert = nrm((DEPTH, D, N_EXPERTS), D ** -0.5)
    moe_b_expert = nrm((DEPTH, N_EXPERTS), 0.01)
    moe_w_gate = nrm((DEPTH, N_EXPERTS, D, D_EXPERT), D ** -0.5)
    moe_w_up = nrm((DEPTH, N_EXPERTS, D, D_EXPERT), D ** -0.5)
    moe_w_down = nrm((DEPTH, N_EXPERTS, D_EXPERT, D), DN_BETA * D_EXPERT ** -0.5)
    return {'x': x, 'c': c, 'ctx': ctx, 'c_ctx': c_ctx, 'w_mod': w_mod, 'b_mod': b_mod,
            'ln_g': ln_g, 'ln_b': ln_b, 'ev_w_in': ev_w_in, 'gla_w_a2': gla_w_a2, 'gla_b_a': gla_b_a,
            'gla_norm': gla_norm, 'lru_conv_w': lru_conv_w, 'lru_conv_b': lru_conv_b,
            'lru_w_r': lru_w_r, 'lru_b_r': lru_b_r, 'lru_w_i': lru_w_i, 'lru_b_i': lru_b_i,
            'lru_lam': lru_lam, 'ev_w_out': ev_w_out, 'od_w_in': od_w_in, 'mlstm_b_gate': mlstm_b_gate,
            'mlstm_norm': mlstm_norm, 'od_w_out': od_w_out, 'moe_w_group': moe_w_group,
            'moe_b_group': moe_b_group, 'moe_w_expert': moe_w_expert, 'moe_b_expert': moe_b_expert,
            'moe_w_gate': moe_w_gate, 'moe_w_up': moe_w_up, 'moe_w_down': moe_w_down}


def reference(x, c, ctx, c_ctx, w_mod, b_mod, ln_g, ln_b, ev_w_in, gla_w_a2, gla_b_a, gla_norm,
              lru_conv_w, lru_conv_b, lru_w_r, lru_b_r, lru_w_i, lru_b_i, lru_lam, ev_w_out,
              od_w_in, mlstm_b_gate, mlstm_norm, od_w_out, moe_w_group, moe_b_group,
              moe_w_expert, moe_b_expert, moe_w_gate, moe_w_up, moe_w_down):
    B, S, D = x.shape
    h_ctx = ctx
    for l in range(DEPTH):
        last = l == DEPTH - 1
        j = l // 2
        m_lat = cond_modulation(c, w_mod[l], b_mod[l])
        m_ctx = cond_modulation(c_ctx[None], w_mod[l], b_mod[l])
        u_lat = modulate(x, m_lat, 0)
        u_ctx = modulate(h_ctx, m_ctx, 0)
        if l % 2 == 0:
            y_lat, y_ctx = even_mixer(u_lat, u_ctx, ev_w_in[j], gla_w_a2[j], gla_b_a[j], gla_norm[j],
                                      lru_conv_w[j], lru_conv_b[j], lru_w_r[j], lru_b_r[j],
                                      lru_w_i[j], lru_b_i[j], lru_lam[j], ev_w_out[j], not last)
        else:
            y_lat, y_ctx = odd_mixer(u_lat, u_ctx, od_w_in[j], mlstm_b_gate[j], mlstm_norm[j],
                                     od_w_out[j], not last)
        x = layer_norm(DN_ALPHA * x + m_lat[:, 2, None] * y_lat, ln_g[l, 0], ln_b[l, 0])
        moe_args = (moe_w_group[l], moe_b_group[l], moe_w_expert[l], moe_b_expert[l],
                    moe_w_gate[l], moe_w_up[l], moe_w_down[l])
        if last:
            y_lat = hier_moe(modulate(x, m_lat, 3).reshape(B * S, D), *moe_args).reshape(B, S, D)
        else:
            h_ctx = layer_norm(DN_ALPHA * h_ctx + m_ctx[:, 2, None] * y_ctx, ln_g[l, 0], ln_b[l, 0])
            tokens = jnp.concatenate([modulate(x, m_lat, 3).reshape(B * S, D),
                                      modulate(h_ctx, m_ctx, 3).reshape(-1, D)], 0)
            y_all = hier_moe(tokens, *moe_args)
            y_lat = y_all[:B * S].reshape(B, S, D)
            y_ctx = y_all[B * S:].reshape(h_ctx.shape)
            h_ctx = layer_norm(DN_ALPHA * h_ctx + m_ctx[:, 5, None] * y_ctx, ln_g[l, 1], ln_b[l, 1])
        x = layer_norm(DN_ALPHA * x + m_lat[:, 5, None] * y_lat, ln_g[l, 1], ln_b[l, 1])
    return x
```

```python
import functools

import jax
import jax.numpy as jnp
from jax import lax
from jax.experimental import pallas as pl
from jax.experimental.pallas import tpu as pltpu

F32 = jnp.float32
BF16 = jnp.bfloat16
HIGHEST = lax.Precision.HIGHEST

D = 2048
DEPTH = 2
GRID_W = 64
CHUNK = 64
GLA_H, GLA_DK, GLA_DV, GLA_RANK, GLA_TAU = 4, 128, 256, 16, 16.0
GLA_QK, GLA_V = GLA_H * GLA_DK, GLA_H * GLA_DV
LRU_W, LRU_NB, LRU_C, CONV_W = 1024, 8, 8.0, 4
LRU_BW = LRU_W // LRU_NB
ML_H, ML_DK, ML_DV = 8, 128, 256
ML_QK, ML_V = ML_H * ML_DK, ML_H * ML_DV
M_INIT = -1e30
N_GROUPS, EPG, N_EXP, D_EXP = 4, 8, 32, 512
ALPHA = (2 * DEPTH) ** 0.25
EPS = 1e-5

LANES = 128
SUBLANES = 8
VMEM_LIMIT = 56 * 1024 * 1024

EV_NP = 5376
EV_TN = 768
EV_GATE_COL = 5120
OD_NP = 6400
OD_TN = 1280
OD_GATE_COL = 6144
MOE_BLK = 256


def _params(n_axes):
    return pltpu.CompilerParams(dimension_semantics=("arbitrary",) * n_axes, vmem_limit_bytes=VMEM_LIMIT)


def _sigmoid(x):
    return 1.0 / (1.0 + jnp.exp(-x))


def _silu(x):
    return x * _sigmoid(x)


def _log_sigmoid(x):
    return jnp.minimum(x, 0.0) - jnp.log1p(jnp.exp(-jnp.abs(x)))


def _gelu_tanh(x):
    return 0.5 * x * (1.0 + jnp.tanh(0.7978845608028654 * (x + 0.044715 * (x * x * x))))


def _bdot(a, b):
    return jnp.dot(a.astype(BF16), b.astype(BF16), preferred_element_type=F32)


def _bdot_nt(a, b):
    return lax.dot_general(a.astype(BF16), b.astype(BF16), (((1,), (1,)), ((), ())), preferred_element_type=F32)


def _bdot_tn(a, b):
    return lax.dot_general(a.astype(BF16), b.astype(BF16), (((0,), (0,)), ((), ())), preferred_element_type=F32)


def _hdot(a, b):
    return jnp.dot(a, b, preferred_element_type=F32, precision=HIGHEST)


def _hdot_nt(a, b):
    return lax.dot_general(a, b, (((1,), (1,)), ((), ())), preferred_element_type=F32, precision=HIGHEST)


def _mod_kernel(c_ref, w_ref, b_ref, o_ref):
    o_ref[...] = _bdot(_silu(c_ref[...]), w_ref[...]) + b_ref[...]


def cond_mod(cond8, w_mod, b_mod):
    nl, _, n6 = w_mod.shape
    tn = 1024
    return pl.pallas_call(
        _mod_kernel,
        grid=(nl, n6 // tn),
        in_specs=[
            pl.BlockSpec((SUBLANES, D), lambda l, j: (0, 0)),
            pl.BlockSpec((None, D, tn), lambda l, j: (l, 0, j)),
            pl.BlockSpec((None, 1, tn), lambda l, j: (l, 0, j)),
        ],
        out_specs=pl.BlockSpec((None, SUBLANES, tn), lambda l, j: (l, 0, j)),
        out_shape=jax.ShapeDtypeStruct((nl, SUBLANES, n6), F32),
        compiler_params=_params(2),
        name="cond_mod",
    )(cond8, w_mod, b_mod.reshape(nl, 1, n6))


def _inproj_kernel(x_ref, m_ref, w_ref, o_ref, u_ref, *, tr, nslab):
    @pl.when(pl.program_id(2) == 0)
    def _():
        shift = m_ref[0:1, :]
        scale = 1.0 + m_ref[1:2, :]
        for s in range(nslab):
            u_ref[s * tr:(s + 1) * tr, :] = (x_ref[:, s * D:(s + 1) * D] * scale + shift).astype(BF16)

    o_ref[...] = jnp.dot(u_ref[...], w_ref[...], preferred_element_type=F32)


def inproj(x3, mvec, w, *, tn, colmajor, tm=1024):
    B, T, _ = x3.shape
    Np = w.shape[1]
    if colmajor:
        R = T // GRID_W
        nslab = max(1, min(GRID_W, tm // R))
        tr, n_mt = R, GRID_W // nslab
        x_view = x3.reshape(B, R, GRID_W * D)
        x_spec = pl.BlockSpec((None, R, nslab * D), lambda b, i, j: (b, 0, i))
    else:
        tr, nslab = min(tm, T), 1
        n_mt = T // tr
        x_view = x3
        x_spec = pl.BlockSpec((None, tr, D), lambda b, i, j: (b, i, 0))
    tmm = tr * nslab
    return pl.pallas_call(
        functools.partial(_inproj_kernel, tr=tr, nslab=nslab),
        grid=(B, n_mt, Np // tn),
        in_specs=[
            x_spec,
            pl.BlockSpec((None, SUBLANES, D), lambda b, i, j: (b, 0, 0)),
            pl.BlockSpec((D, tn), lambda b, i, j: (0, j)),
        ],
        out_specs=pl.BlockSpec((None, tmm, tn), lambda b, i, j: (b, i, j)),
        out_shape=jax.ShapeDtypeStruct((B, T, Np), F32),
        scratch_shapes=[pltpu.VMEM((tmm, D), BF16)],
        compiler_params=_params(3),
        name="inproj",
    )(x_view, mvec, w)


def _gla_kernel(qf, kf, vf, gf, qb, kb, vb, gb, wa_ref, ba_ref, s0_ref, of_ref, ob_ref, sfin_ref, st_ref, *, tb, ns):
    s = pl.program_id(1)

    @pl.when(s == 0)
    def _():
        st_ref[...] = s0_ref[...]

    L = CHUNK
    row = lax.broadcasted_iota(jnp.int32, (L, L), 0)
    col = lax.broadcasted_iota(jnp.int32, (L, L), 1)
    scale = GLA_DK ** -0.5
    for d, (q_ref, k_ref, v_ref, g_ref, o_ref) in enumerate(((qf, kf, vf, gf, of_ref), (qb, kb, vb, gb, ob_ref))):
        mask = (row >= col) if d == 0 else (row <= col)
        mf = mask.astype(F32)
        glog = _log_sigmoid(_hdot(g_ref[...], wa_ref[d]) + ba_ref[d]) * (1.0 / GLA_TAU)
        nchunk = tb // L
        for c in (range(nchunk) if d == 0 else range(nchunk - 1, -1, -1)):
            r0 = c * L
            for h in range(GLA_H):
                q = q_ref[r0:r0 + L, h * GLA_DK:(h + 1) * GLA_DK]
                k = k_ref[r0:r0 + L, h * GLA_DK:(h + 1) * GLA_DK]
                v = v_ref[r0:r0 + L, h * GLA_DV:(h + 1) * GLA_DV].astype(BF16)
                g = glog[r0:r0 + L, h * GLA_DK:(h + 1) * GLA_DK]
                b = _hdot(mf, g)
                bl = b[L - 1:L, :] if d == 0 else b[0:1, :]
                qd = (q * (jnp.exp(b) * scale)).astype(BF16)
                ki = k * jnp.exp(-b)
                ke = k * jnp.exp(bl - b)
                att = jnp.where(mask, _bdot_nt(qd, ki), 0.0)
                st = st_ref[d, h]
                o = _bdot(att, v) + _bdot_nt(qd, st)
                o_ref[r0:r0 + L, h * GLA_DV:(h + 1) * GLA_DV] = o
                st_ref[d, h] = st * jnp.exp(bl) + _bdot_tn(v, ke)

    @pl.when(s == ns - 1)
    def _():
        sfin_ref[...] = st_ref[...]


def gla_scan(P, wa, ba, s0, *, tb=256):
    B, T, _ = P.shape
    tb = min(tb, T)
    ns = T // tb
    fwd = lambda b, s: s
    bwd = lambda b, s: ns - 1 - s

    def specs(pos):
        return [
            pl.BlockSpec((None, tb, GLA_QK), lambda b, s: (b, pos(b, s), 0)),
            pl.BlockSpec((None, tb, GLA_QK), lambda b, s: (b, pos(b, s), 1)),
            pl.BlockSpec((None, tb, GLA_V), lambda b, s: (b, pos(b, s), 1)),
            pl.BlockSpec((None, tb, LANES), lambda b, s: (b, pos(b, s), EV_GATE_COL // LANES)),
        ]

    st_shape = (2, GLA_H, GLA_DV, GLA_DK)
    st_spec = pl.BlockSpec((None,) + st_shape, lambda b, s: (b, 0, 0, 0, 0))
    return pl.pallas_call(
        functools.partial(_gla_kernel, tb=tb, ns=ns),
        grid=(B, ns),
        in_specs=specs(fwd) + specs(bwd) + [
            pl.BlockSpec((2, LANES, GLA_QK), lambda b, s: (0, 0, 0)),
            pl.BlockSpec((2, 1, GLA_QK), lambda b, s: (0, 0, 0)),
            st_spec,
        ],
        out_specs=[
            pl.BlockSpec((None, tb, GLA_V), lambda b, s: (b, s, 0)),
            pl.BlockSpec((None, tb, GLA_V), lambda b, s: (b, ns - 1 - s, 0)),
            st_spec,
        ],
        out_shape=[
            jax.ShapeDtypeStruct((B, T, GLA_V), F32),
            jax.ShapeDtypeStruct((B, T, GLA_V), F32),
            jax.ShapeDtypeStruct((B,) + st_shape, F32),
        ],
        scratch_shapes=[pltpu.VMEM(st_shape, F32)],
        compiler_params=_params(2),
        name="gla_scan",
    )(P, P, P, P, P, P, P, P, wa, ba, s0)


def _lru_gate_kernel(x_ref, p_ref, n_ref, cw_ref, cb_ref, w_ref, bias_ref, ls_ref,
                     af_ref, uf_ref, ab_ref, ub_ref, *, tm, nt):
    i = pl.program_id(1)
    prev = jnp.where(i > 0, p_ref[...], 0.0)
    nxt = jnp.where(i < nt - 1, n_ref[...], 0.0)
    ext = jnp.concatenate([prev, x_ref[...], nxt], axis=0)
    cw = cw_ref[...]
    h0 = SUBLANES - CONV_W // 2
    xc = cb_ref[...] + (ext[h0:h0 + tm] * cw[0:1] + ext[h0 + 1:h0 + 1 + tm] * cw[1:2]
                        + ext[h0 + 2:h0 + 2 + tm] * cw[2:3] + ext[h0 + 3:h0 + 3 + tm] * cw[3:4])
    for n in range(LRU_NB):
        cs = slice(n * LRU_BW, (n + 1) * LRU_BW)
        xn = xc[:, cs]
        z = _bdot(xn, w_ref[n]) + bias_ref[n]
        for d, (a_ref, u_ref) in enumerate(((af_ref, uf_ref), (ab_ref, ub_ref))):
            r = _sigmoid(z[:, (2 * d) * LRU_BW:(2 * d + 1) * LRU_BW])
            ig = _sigmoid(z[:, (2 * d + 1) * LRU_BW:(2 * d + 2) * LRU_BW])
            log_a = (LRU_C * r) * _log_sigmoid(ls_ref[d:d + 1, cs])
            a_ref[:, cs] = jnp.exp(log_a)
            u_ref[:, cs] = jnp.sqrt(1.0 - jnp.exp(2.0 * log_a)) * (ig * xn)


def lru_gates(P, conv_w, conv_b, wcat, bcat, lam, *, tm=512):
    B, T, _ = P.shape
    tm = min(tm, T)
    nt = T // tm
    xcol = 3
    r8 = tm // SUBLANES
    out = jax.ShapeDtypeStruct((B, T, LRU_W), F32)
    ospec = pl.BlockSpec((None, tm, LRU_W), lambda b, i: (b, i, 0))
    return pl.pallas_call(
        functools.partial(_lru_gate_kernel, tm=tm, nt=nt),
        grid=(B, nt),
        in_specs=[
            pl.BlockSpec((None, tm, LRU_W), lambda b, i: (b, i, xcol)),
            pl.BlockSpec((None, SUBLANES, LRU_W), lambda b, i: (b, jnp.maximum(i * r8 - 1, 0), xcol)),
            pl.BlockSpec((None, SUBLANES, LRU_W), lambda b, i: (b, jnp.minimum((i + 1) * r8, T // SUBLANES - 1), xcol)),
            pl.BlockSpec((CONV_W, LRU_W), lambda b, i: (0, 0)),
            pl.BlockSpec((1, LRU_W), lambda b, i: (0, 0)),
            pl.BlockSpec((LRU_NB, LRU_BW, 4 * LRU_BW), lambda b, i: (0, 0, 0)),
            pl.BlockSpec((LRU_NB, 1, 4 * LRU_BW), lambda b, i: (0, 0, 0)),
            pl.BlockSpec((2, LRU_W), lambda b, i: (0, 0)),
        ],
        out_specs=[ospec] * 4,
        out_shape=[out] * 4,
        compiler_params=_params(2),
        name="lru_gates",
    )(P, P, P, conv_w, conv_b.reshape(1, LRU_W), wcat, bcat, lam)


def _lru_scan_kernel(af, uf, ab, ub, h0_ref, hf_ref, hb_ref, hfin_ref, h_ref, *, tb, ns):
    s = pl.program_id(1)

    @pl.when(s == 0)
    def _():
        h_ref[...] = h0_ref[...]

    def body(t, carry):
        hf, hb = carry
        hf = af[t] * hf + uf[t]
        hf_ref[t] = hf
        tr = tb - 1 - t
        hb = ab[tr] * hb + ub[tr]
        hb_ref[tr] = hb
        return hf, hb

    hf, hb = lax.fori_loop(0, tb, body, (h_ref[0], h_ref[1]), unroll=8)
    h_ref[0] = hf
    h_ref[1] = hb

    @pl.when(s == ns - 1)
    def _():
        hfin_ref[...] = h_ref[...]


def lru_scan(af, uf, ab, ub, h0, *, tb=512):
    B, T, _ = af.shape
    tb = min(tb, T)
    ns = T // tb
    v4 = lambda a: a.reshape(B, T, SUBLANES, LANES)
    fspec = pl.BlockSpec((None, tb, SUBLANES, LANES), lambda b, s: (b, s, 0, 0))
    bspec = pl.BlockSpec((None, tb, SUBLANES, LANES), lambda b, s: (b, ns - 1 - s, 0, 0))
    hspec = pl.BlockSpec((None, 2, SUBLANES, LANES), lambda b, s: (b, 0, 0, 0))
    seq = jax.ShapeDtypeStruct((B, T, SUBLANES, LANES), F32)
    hf, hb, hfin = pl.pallas_call(
        functools.partial(_lru_scan_kernel, tb=tb, ns=ns),
        grid=(B, ns),
        in_specs=[fspec, fspec, bspec, bspec, hspec],
        out_specs=[fspec, bspec, hspec],
        out_shape=[seq, seq, jax.ShapeDtypeStruct((B, 2, SUBLANES, LANES), F32)],
        scratch_shapes=[pltpu.VMEM((2, SUBLANES, LANES), F32)],
        compiler_params=_params(2),
        name="lru_scan",
    )(v4(af), v4(uf), v4(ab), v4(ub), h0)
    return hf.reshape(B, T, LRU_W), hb.reshape(B, T, LRU_W), hfin


ML_AUG = ML_DV + LANES


def _mlstm_kernel(qf, kf, vf, gf, qb, kb, vb, gb, bg_ref, c0_ref, m0_ref,
                  hf_ref, hb_ref, cfin_ref, mfin_ref, c_ref, m_ref, *, tb, ns):
    s = pl.program_id(2)

    @pl.when(s == 0)
    def _():
        c_ref[...] = c0_ref[...]
        m_ref[...] = m0_ref[...]

    L = CHUNK
    row = lax.broadcasted_iota(jnp.int32, (L, L), 0)
    col = lax.broadcasted_iota(jnp.int32, (L, L), 1)
    eye = (row == col).astype(F32)
    sub = lax.broadcasted_iota(jnp.int32, (SUBLANES, L), 0)
    is_f = (sub == 1) | (sub == 3)
    ones_col = (lax.broadcasted_iota(jnp.int32, (L, LANES), 1) == 0).astype(BF16)
    kscale = ML_DK ** -0.5
    bg = bg_ref[...]
    for d, (q_ref, k_ref, v_ref, g_ref, h_ref) in enumerate(((qf, kf, vf, gf, hf_ref), (qb, kb, vb, gb, hb_ref))):
        mask = (row >= col) if d == 0 else (row <= col)
        mf = mask.astype(F32)
        mtf = ((col >= row) if d == 0 else (col <= row)).astype(F32)
        ri, rf = 2 * d, 2 * d + 1
        nchunk = tb // L
        for c in (range(nchunk) if d == 0 else range(nchunk - 1, -1, -1)):
            r0 = c * L
            pre = g_ref[c] + bg
            xg = jnp.where(is_f, _log_sigmoid(pre), pre)
            brow = _hdot(xg, mtf)
            bcol = _hdot_nt(mf, xg)
            icol = _hdot_nt(eye, xg)
            b_row, i_row = brow[rf:rf + 1, :], xg[ri:ri + 1, :]
            b_col, i_col = bcol[:, rf:rf + 1], icol[:, ri:ri + 1]
            b_last = b_col[L - 1:L, :] if d == 0 else b_col[0:1, :]
            m = m_ref[d][0:1, 0:1]
            dmat = jnp.where(mask, b_col + (i_row - b_row), -jnp.inf)
            inter = b_col + m
            m_t = jnp.maximum(inter, jnp.max(dmat, axis=1, keepdims=True))
            w_intra = jnp.exp(dmat - m_t)
            w_inter = jnp.exp(inter - m_t)
            q = q_ref[r0:r0 + L, :].astype(BF16)
            k = k_ref[r0:r0 + L, :] * kscale
            vaug = jnp.concatenate([v_ref[r0:r0 + L, :].astype(BF16), ones_col], axis=1)
            sc = _bdot_nt(q, k) * w_intra
            caug = c_ref[d]
            acc = _bdot(sc, vaug) + w_inter * _bdot(q, caug)
            den = acc[:, ML_DV:ML_DV + 1]
            h_ref[r0:r0 + L, :] = acc[:, :ML_DV] / jnp.maximum(jnp.abs(den), jnp.exp(-m_t))
            g_col = b_last - b_col + i_col
            m_new = jnp.maximum(b_last + m, jnp.max(g_col, axis=0, keepdims=True))
            wk = jnp.exp(g_col - m_new)
            decay = jnp.exp(b_last + m - m_new)
            c_ref[d] = decay * caug + _bdot_tn(wk * k, vaug)
            m_ref[d] = jnp.broadcast_to(m_new, (SUBLANES, LANES))

    @pl.when(s == ns - 1)
    def _():
        cfin_ref[...] = c_ref[...]
        mfin_ref[...] = m_ref[...]


def mlstm_scan(P, gt, bgate, c0, m0, *, tb=256):
    B, T, _ = P.shape
    tb = min(tb, T)
    ns = T // tb
    ncb = tb // CHUNK
    fwd = lambda s: s
    bwd = lambda s: ns - 1 - s

    def specs(pos):
        return [
            pl.BlockSpec((None, tb, ML_DK), lambda b, h, s: (b, pos(s), h)),
            pl.BlockSpec((None, tb, ML_DK), lambda b, h, s: (b, pos(s), ML_H + h)),
            pl.BlockSpec((None, tb, ML_DV), lambda b, h, s: (b, pos(s), ML_H + h)),
            pl.BlockSpec((None, None, ncb, SUBLANES, CHUNK), lambda b, h, s: (b, h, pos(s), 0, 0)),
        ]

    cshape = (2, ML_DK, ML_AUG)
    mshape = (2, SUBLANES, LANES)
    cspec = pl.BlockSpec((None, 2, None, ML_DK, ML_AUG), lambda b, h, s: (b, 0, h, 0, 0))
    mspec = pl.BlockSpec((None, 2, None, SUBLANES, LANES), lambda b, h, s: (b, 0, h, 0, 0))
    return pl.pallas_call(
        functools.partial(_mlstm_kernel, tb=tb, ns=ns),
        grid=(B, ML_H, ns),
        in_specs=specs(fwd) + specs(bwd) + [
            pl.BlockSpec((None, SUBLANES, 1), lambda b, h, s: (h, 0, 0)),
            cspec, mspec,
        ],
        out_specs=[
            pl.BlockSpec((None, tb, ML_DV), lambda b, h, s: (b, s, h)),
            pl.BlockSpec((None, tb, ML_DV), lambda b, h, s: (b, ns - 1 - s, h)),
            cspec, mspec,
        ],
        out_shape=[
            jax.ShapeDtypeStruct((B, T, ML_V), F32),
            jax.ShapeDtypeStruct((B, T, ML_V), F32),
            jax.ShapeDtypeStruct((B, 2, ML_H, ML_DK, ML_AUG), F32),
            jax.ShapeDtypeStruct((B, 2, ML_H, SUBLANES, LANES), F32),
        ],
        scratch_shapes=[pltpu.VMEM(cshape, F32), pltpu.VMEM(mshape, F32)],
        compiler_params=_params(3),
        name="mlstm_scan",
    )(P, P, P, gt, P, P, P, gt, bgate, c0, m0)


def _layer_norm(z, g, b):
    mu = jnp.mean(z, axis=-1, keepdims=True)
    zc = z - mu
    var = jnp.mean(zc * zc, axis=-1, keepdims=True)
    return zc * lax.rsqrt(var + EPS) * g + b


def _head_rms(o, g, n_heads, width):
    parts = []
    for h in range(n_heads):
        oh = o[:, h * width:(h + 1) * width]
        parts.append(oh * lax.rsqrt(jnp.mean(oh * oh, axis=-1, keepdims=True) + EPS) * g)
    return jnp.concatenate(parts, axis=-1)


def _post(y, x_ref, m_ref, lng_ref, lnb_ref, rw_refs, x1_ref, xm_ref, eid_ref, gate_ref):
    wgh, wgl, weh, wel, bgr, ber = rw_refs
    x1 = _layer_norm(ALPHA * x_ref[...] + m_ref[2:3, :] * y, lng_ref[...], lnb_ref[...])
    x1_ref[...] = x1
    xm = x1 * (1.0 + m_ref[4:5, :]) + m_ref[3:4, :]
    xm_ref[...] = xm
    hi = xm.astype(BF16)
    lo = (xm - hi.astype(F32)).astype(BF16)

    def logits(wh, wl, bias):
        return (jnp.dot(hi, wh[...], preferred_element_type=F32) + jnp.dot(hi, wl[...], preferred_element_type=F32)
                + jnp.dot(lo, wh[...], preferred_element_type=F32)) + bias[...]

    gl = logits(wgh, wgl, bgr)
    el = logits(weh, wel, ber)
    lane = lax.broadcasted_iota(jnp.int32, gl.shape, 1)
    neg = -jnp.inf
    glm = jnp.where(lane < N_GROUPS, gl, neg)
    gmax = jnp.max(glm, axis=-1, keepdims=True)
    gidx = jnp.min(jnp.where(glm == gmax, lane, LANES), axis=-1, keepdims=True)
    p_g = 1.0 / jnp.sum(jnp.exp(glm - gmax), axis=-1, keepdims=True)
    in_group = (lax.shift_right_logical(lane, 3) == gidx) & (lane < N_EXP)
    elm = jnp.where(in_group, el, neg)
    m1 = jnp.max(elm, axis=-1, keepdims=True)
    i1 = jnp.min(jnp.where(elm == m1, lane, LANES), axis=-1, keepdims=True)
    elm2 = jnp.where(lane == i1, neg, elm)
    m2 = jnp.max(elm2, axis=-1, keepdims=True)
    i2 = jnp.min(jnp.where(elm2 == m2, lane, LANES), axis=-1, keepdims=True)
    e2 = jnp.exp(m2 - m1)
    g1 = p_g / (1.0 + e2)
    g2 = g1 * e2
    eid_ref[...] = jnp.where(lane == 0, i1, jnp.where(lane == 1, i2, 0))
    gate_ref[...] = jnp.where(lane == 0, g1, jnp.where(lane == 1, g2, 0.0))


def _out_even_kernel(of, ob, r, hf, hb, xg, x, m, gg, wout, lng, lnb, wgh, wgl, weh, wel, bgr, ber, *rest):
    x1_ref, xm_ref, eid_ref, gate_ref = rest[-4:]
    o = _head_rms(of[...] + ob[...], gg[...], GLA_H, GLA_DV) * _silu(r[...])
    y2 = (hf[...] + hb[...]) * _gelu_tanh(xg[...])
    mix = jnp.concatenate([o.astype(BF16), y2.astype(BF16)], axis=-1)
    y = jnp.dot(mix, wout[...], preferred_element_type=F32)
    _post(y, x, m, lng, lnb, (wgh, wgl, weh, wel, bgr, ber), x1_ref, xm_ref, eid_ref, gate_ref)


def _out_odd_kernel(hf, hb, og, x, m, gg, wout, lng, lnb, wgh, wgl, weh, wel, bgr, ber, *rest):
    x1_ref, xm_ref, eid_ref, gate_ref = rest[-4:]
    o = _head_rms(hf[...] + hb[...], gg[...], ML_H, ML_DV) * _sigmoid(og[...])
    y = jnp.dot(o.astype(BF16), wout[...], preferred_element_type=F32)
    _post(y, x, m, lng, lnb, (wgh, wgl, weh, wel, bgr, ber), x1_ref, xm_ref, eid_ref, gate_ref)


def _router_weights(w_group, b_group, w_expert, b_expert):
    def split(w, n):
        wp = jnp.zeros((D, LANES), F32).at[:, :n].set(w)
        hi = wp.astype(BF16)
        return hi, (wp - hi.astype(F32)).astype(BF16)

    def padb(b, n):
        return jnp.zeros((1, LANES), F32).at[0, :n].set(b)

    wgh, wgl = split(w_group, N_GROUPS)
    weh, wel = split(w_expert, N_EXP)
    return wgh, wgl, weh, wel, padb(b_group, N_GROUPS), padb(b_expert, N_EXP)


def _const_spec(shape):
    nd = len(shape)
    return pl.BlockSpec(shape, lambda *_: (0,) * nd)


def out_block(kind, acts, x, mvec, gg, wout, lng, lnb, rw, *, n_tok_all, row_off, carry=None, colmajor=False, tm=256):
    B, T, _ = x.shape
    tm = min(tm, T)
    if colmajor:
        R = tm = T // GRID_W
        nt = GRID_W
        tok = lambda width: pl.BlockSpec((None, R, width), lambda b, i: (b, 0, i))
        x_in = x.reshape(B, R, GRID_W * D)
        view = lambda a, width: a.reshape(B, R, GRID_W * width)
        assert row_off == 0 and n_tok_all == B * T
    else:
        nt = T // tm
        tok = lambda width: pl.BlockSpec((None, tm, width), lambda b, i: (b, i, 0))
        x_in = x
    act = lambda width, cb=0: pl.BlockSpec((None, tm, width), lambda b, i: (b, i, cb))
    if kind == "even":
        of, ob, P, hf, hb = acts
        a_in = [of, ob, P, hf, hb, P]
        a_specs = [act(GLA_V), act(GLA_V), act(GLA_V, 2), act(LRU_W), act(LRU_W), act(LRU_W, 4)]
        body = _out_even_kernel
    else:
        hf, hb, P = acts
        a_in = [hf, hb, P]
        a_specs = [act(ML_V), act(ML_V), act(ML_V, 2)]
        body = _out_odd_kernel
    w_in = [mvec, gg.reshape(1, -1), wout, lng.reshape(1, D), lnb.reshape(1, D)] + list(rw)
    w_specs = [pl.BlockSpec((None, SUBLANES, D), lambda b, i: (b, 0, 0))] + [_const_spec(a.shape) for a in w_in[1:]]
    if colmajor:
        tab = lambda width: pl.BlockSpec((None, R, width), lambda b, i: (b, 0, i))
        tab_shape = lambda width, dt: jax.ShapeDtypeStruct((B, R, GRID_W * width), dt)
    else:
        rb = row_off // tm
        tab = lambda width: pl.BlockSpec((tm, width), lambda b, i: (rb + b * nt + i, 0))
        tab_shape = lambda width, dt: jax.ShapeDtypeStruct((n_tok_all, width), dt)
    out_specs = [tok(D), tab(D), tab(LANES), tab(LANES)]
    out_shape = [jax.ShapeDtypeStruct(x_in.shape, F32), tab_shape(D, F32), tab_shape(LANES, jnp.int32), tab_shape(LANES, F32)]
    c_in, c_specs, aliases = [], [], {}
    if carry is None and n_tok_all != B * T:
        carry = (jnp.zeros((n_tok_all, D), F32), jnp.zeros((n_tok_all, LANES), jnp.int32), jnp.zeros((n_tok_all, LANES), F32))
    if carry is not None:
        c_in = list(carry)
        c_specs = [pl.BlockSpec(memory_space=pl.ANY)] * 3
        base = len(a_in) + 1 + len(w_in)
        aliases = {base: 1, base + 1: 2, base + 2: 3}
    x1, xm, eid, gate = pl.pallas_call(
        body,
        grid=(B, nt),
        in_specs=a_specs + [tok(D)] + w_specs + c_specs,
        out_specs=out_specs,
        out_shape=out_shape,
        input_output_aliases=aliases,
        compiler_params=_params(2),
        name="out_" + kind,
    )(*a_in, x_in, *w_in, *c_in)
    if colmajor:
        x1 = x1.reshape(B, T, D)
        xm, eid, gate = xm.reshape(B * T, D), eid.reshape(B * T, LANES), gate.reshape(B * T, LANES)
    return x1, xm, eid, gate


def _expert_kernel(be_ref, src_cur, src_nxt, dst_cur, g_ref, wg_ref, wu_ref, wd_ref, xm_hbm, y_hbm,
                   xbuf, ybuf, gsem, ssem, *, nb):
    del be_ref
    i = pl.program_id(0)
    slot = lax.rem(i, 2)
    blk = MOE_BLK

    def gather(idx_ref, buf_slot):
        def body(r, c):
            pltpu.make_async_copy(xm_hbm.at[pl.ds(idx_ref[0, r], 1)], xbuf.at[buf_slot, pl.ds(r, 1)], gsem.at[buf_slot]).start()
            return c
        lax.fori_loop(0, blk, body, 0, unroll=8)

    def wait_gather(buf_slot):
        pltpu.make_async_copy(xm_hbm.at[pl.ds(0, blk)], xbuf.at[buf_slot], gsem.at[buf_slot]).wait()

    def wait_scatter(buf_slot):
        pltpu.make_async_copy(ybuf.at[buf_slot], y_hbm.at[pl.ds(0, blk)], ssem.at[buf_slot]).wait()

    @pl.when(i == 0)
    def _():
        gather(src_cur, 0)

    @pl.when(i + 1 < nb)
    def _():
        gather(src_nxt, 1 - slot)

    wait_gather(slot)

    @pl.when(i >= 2)
    def _():
        wait_scatter(slot)

    xb = xbuf[slot].astype(BF16)
    hg = jnp.dot(xb, wg_ref[...], preferred_element_type=F32)
    hu = jnp.dot(xb, wu_ref[...], preferred_element_type=F32)
    h = (_silu(hg) * hu).astype(BF16)
    ybuf[slot] = jnp.dot(h, wd_ref[...], preferred_element_type=F32) * g_ref[...]

    def scatter(r, c):
        pltpu.make_async_copy(ybuf.at[slot, pl.ds(r, 1)], y_hbm.at[pl.ds(dst_cur[0, r], 1)], ssem.at[slot]).start()
        return c
    lax.fori_loop(0, blk, scatter, 0, unroll=8)

    @pl.when(i == nb - 1)
    def _():
        wait_scatter(slot)
        if nb >= 2:
            wait_scatter(1 - slot)


def moe_experts(xm, block_e, slot_src, slot_dst, slot_gate, wg, wu, wd, n_rows_out):
    nb = block_e.shape[0]
    blk = MOE_BLK
    src3 = slot_src.reshape(nb, 1, blk)
    dst3 = slot_dst.reshape(nb, 1, blk)
    smem_blk = lambda f: pl.BlockSpec((None, 1, blk), f, memory_space=pltpu.SMEM)
    grid_spec = pltpu.PrefetchScalarGridSpec(
        num_scalar_prefetch=1,
        grid=(nb,),
        in_specs=[
            smem_blk(lambda i, be: (i, 0, 0)),
            smem_blk(lambda i, be: (jnp.minimum(i + 1, nb - 1), 0, 0)),
            smem_blk(lambda i, be: (i, 0, 0)),
            pl.BlockSpec((None, blk, 1), lambda i, be: (i, 0, 0)),
            pl.BlockSpec((None, D, D_EXP), lambda i, be: (be[i], 0, 0)),
            pl.BlockSpec((None, D, D_EXP), lambda i, be: (be[i], 0, 0)),
            pl.BlockSpec((None, D_EXP, D), lambda i, be: (be[i], 0, 0)),
            pl.BlockSpec(memory_space=pl.ANY),
        ],
        out_specs=pl.BlockSpec(memory_space=pl.ANY),
        scratch_shapes=[
            pltpu.VMEM((2, blk, D), F32),
            pltpu.VMEM((2, blk, D), F32),
            pltpu.SemaphoreType.DMA((2,)),
            pltpu.SemaphoreType.DMA((2,)),
        ],
    )
    return pl.pallas_call(
        functools.partial(_expert_kernel, nb=nb),
        grid_spec=grid_spec,
        out_shape=jax.ShapeDtypeStruct((n_rows_out, D), F32),
        compiler_params=_params(1),
        name="moe_experts",
    )(block_e, src3, src3, dst3, slot_gate.reshape(nb, blk, 1), wg, wu, wd, xm)


def moe_tables(eid, gate, n_tok):
    blk = MOE_BLK
    n_assign = 2 * n_tok
    flat_e = eid[:, :2].reshape(-1)
    flat_g = gate[:, :2].reshape(-1)
    counts = jnp.sum((flat_e[:, None] == jnp.arange(N_EXP, dtype=jnp.int32)[None, :]).astype(jnp.int32), axis=0)
    padded = (counts + blk - 1) // blk * blk
    pad_end = jnp.cumsum(padded)
    pad_start = pad_end - padded
    cnt_start = jnp.cumsum(counts) - counts
    order = jnp.argsort(flat_e).astype(jnp.int32)
    nb = -(-(n_assign + N_EXP * (blk - 1)) // blk)
    n_slots = nb * blk
    block_e = jnp.minimum(jnp.searchsorted(pad_end, jnp.arange(nb, dtype=jnp.int32) * blk, side="right"),
                          N_EXP - 1).astype(jnp.int32)
    slot = jnp.arange(n_slots, dtype=jnp.int32)
    e_s = jnp.repeat(block_e, blk)
    j = slot - pad_start[e_s]
    valid = j < counts[e_s]
    a = order[jnp.clip(cnt_start[e_s] + j, 0, n_assign - 1)]
    slot_src = jnp.where(valid, a // 2, 0).astype(jnp.int32)
    n_valid_before = jnp.cumsum(valid.astype(jnp.int32)) - valid.astype(jnp.int32)
    slot_dst = jnp.where(valid, a, n_assign + slot - n_valid_before).astype(jnp.int32)
    slot_gate = jnp.where(valid, flat_g[a], 0.0)
    return block_e, slot_src, slot_dst, slot_gate, n_slots


def _combine_kernel(x_ref, y_ref, m_ref, g_ref, b_ref, o_ref):
    y = y_ref[:, :D] + y_ref[:, D:]
    o_ref[...] = _layer_norm(ALPHA * x_ref[...] + m_ref[5:6, :] * y, g_ref[...], b_ref[...])


def moe_combine(x1, Y, mvec, lng, lnb, *, row_off, tm=512):
    B, T, _ = x1.shape
    tm = min(tm, T)
    nt = T // tm
    rb = row_off // tm
    y2 = Y.reshape(Y.shape[0] // 2, 2 * D)
    return pl.pallas_call(
        _combine_kernel,
        grid=(B, nt),
        in_specs=[
            pl.BlockSpec((None, tm, D), lambda b, i: (b, i, 0)),
            pl.BlockSpec((tm, 2 * D), lambda b, i: (rb + b * nt + i, 0)),
            pl.BlockSpec((None, SUBLANES, D), lambda b, i: (b, 0, 0)),
            _const_spec((1, D)), _const_spec((1, D)),
        ],
        out_specs=pl.BlockSpec((None, tm, D), lambda b, i: (b, i, 0)),
        out_shape=jax.ShapeDtypeStruct((B, T, D), F32),
        compiler_params=_params(2),
        name="moe_combine",
    )(x1, y2, mvec, lng.reshape(1, D), lnb.reshape(1, D))


def _even_w_in(w):
    g0 = GLA_QK * 2 + GLA_V * 2
    g1 = g0 + 2 * GLA_RANK
    pad = jnp.zeros((D, EV_NP - w.shape[1]), w.dtype)
    return jnp.concatenate([w[:, :g0], w[:, g1:], w[:, g0:g1], pad], axis=1).astype(BF16)


def _odd_w_in(w):
    return jnp.concatenate([w, jnp.zeros((D, OD_NP - w.shape[1]), w.dtype)], axis=1).astype(BF16)


def _lru_weights(w_r, b_r, w_i, b_i):
    wcat = jnp.concatenate([w_r[0], w_i[0], w_r[1], w_i[1]], axis=-1).astype(BF16)
    blk = lambda b: b.reshape(LRU_NB, 1, LRU_BW)
    bcat = jnp.concatenate([blk(b_r[0]), blk(b_i[0]), blk(b_r[1]), blk(b_i[1])], axis=-1)
    return wcat, bcat


def _gla_gate_weights(w_a2, b_a):
    wa = jnp.zeros((2, LANES, GLA_QK), F32)
    wa = wa.at[0, :GLA_RANK].set(w_a2[0]).at[1, GLA_RANK:2 * GLA_RANK].set(w_a2[1])
    return wa, b_a.reshape(2, 1, GLA_QK)


def _mlstm_gate_table(P, T):
    B = P.shape[0]
    g = P[:, :, OD_GATE_COL:OD_GATE_COL + 4 * ML_H].reshape(B, T // CHUNK, CHUNK, 4, ML_H)
    g = jnp.transpose(g, (0, 4, 1, 3, 2))
    return jnp.concatenate([g, jnp.zeros_like(g)], axis=3)


def _mvec(mods_l, rows):
    m6 = jnp.stack([mods_l[r] for r in rows]).reshape(len(rows), 6, D)
    return jnp.concatenate([m6, jnp.zeros((len(rows), 2, D), F32)], axis=1)


def kernel(x, c, ctx, c_ctx, w_mod, b_mod, ln_g, ln_b, ev_w_in, gla_w_a2, gla_b_a, gla_norm, lru_conv_w, lru_conv_b, lru_w_r, lru_b_r, lru_w_i, lru_b_i, lru_lam, ev_w_out, od_w_in, mlstm_b_gate, mlstm_norm, od_w_out, moe_w_group, moe_b_group, moe_w_expert, moe_b_expert, moe_w_gate, moe_w_up, moe_w_down):
    B, S, _ = x.shape
    T_ctx = ctx.shape[1]
    cond8 = jnp.zeros((SUBLANES, D), F32).at[:B].set(c).at[B].set(c_ctx)
    mods = cond_mod(cond8, w_mod, b_mod)
    h_ctx = ctx
    for l in range(DEPTH):
        j = l // 2
        last = l == DEPTH - 1
        m_lat = _mvec(mods[l], list(range(B)))
        m_ctx = _mvec(mods[l], [B] * B)
        rw = _router_weights(moe_w_group[l], moe_b_group[l], moe_w_expert[l], moe_b_expert[l])
        n_tok = B * S if last else B * (S + T_ctx)
        if l % 2 == 0:
            w_in = _even_w_in(ev_w_in[j])
            wa, ba = _gla_gate_weights(gla_w_a2[j], gla_b_a[j])
            wcat, bcat = _lru_weights(lru_w_r[j], lru_b_r[j], lru_w_i[j], lru_b_i[j])
            w_out = ev_w_out[j].astype(BF16)

            def mix_even(xin, mv, s0, h0):
                P = inproj(xin, mv, w_in, tn=EV_TN, colmajor=False)
                of, ob, sfin = gla_scan(P, wa, ba, s0)
                af, uf, ab, ub = lru_gates(P, lru_conv_w[j], lru_conv_b[j], wcat, bcat, lru_lam[j])
                hf, hb, hfin = lru_scan(af, uf, ab, ub, h0)
                return (of, ob, P, hf, hb), sfin, hfin

            s0 = jnp.zeros((B, 2, GLA_H, GLA_DV, GLA_DK), F32)
            h0 = jnp.zeros((B, 2, SUBLANES, LANES), F32)
            acts_c, s_c, h_c = mix_even(h_ctx, m_ctx, s0, h0)
            acts_l, _, _ = mix_even(x, m_lat, s_c, h_c)
            carry = None
            if not last:
                hc1, *carry = out_block("even", acts_c, h_ctx, m_ctx, gla_norm[j], w_out, ln_g[l, 0], ln_b[l, 0], rw,
                                        n_tok_all=n_tok, row_off=B * S)
            x1, xm, eid, gate = out_block("even", acts_l, x, m_lat, gla_norm[j], w_out, ln_g[l, 0], ln_b[l, 0], rw,
                                          n_tok_all=n_tok, row_off=0, carry=carry)
        else:
            w_in = _odd_w_in(od_w_in[j])
            w_out = od_w_out[j].astype(BF16)
            bgate = jnp.concatenate([mlstm_b_gate[j].T, jnp.zeros((ML_H, 4), F32)], axis=1).reshape(ML_H, SUBLANES, 1)

            def mix_odd(xin, mv, c0, m0, colmajor):
                P = inproj(xin, mv, w_in, tn=OD_TN, colmajor=colmajor)
                gt = _mlstm_gate_table(P, xin.shape[1])
                hf, hb, cfin, mfin = mlstm_scan(P, gt, bgate, c0, m0)
                return (hf, hb, P), cfin, mfin

            c0 = jnp.zeros((B, 2, ML_H, ML_DK, ML_AUG), F32)
            m0 = jnp.full((B, 2, ML_H, SUBLANES, LANES), M_INIT, F32)
            acts_c, c_c, m_c = mix_odd(h_ctx, m_ctx, c0, m0, False)
            acts_l, _, _ = mix_odd(x, m_lat, c_c, m_c, True)
            carry = None
            if not last:
                hc1, *carry = out_block("odd", acts_c, h_ctx, m_ctx, mlstm_norm[j], w_out, ln_g[l, 0], ln_b[l, 0], rw,
                                        n_tok_all=n_tok, row_off=B * S)
            x1, xm, eid, gate = out_block("odd", acts_l, x, m_lat, mlstm_norm[j], w_out, ln_g[l, 0], ln_b[l, 0], rw,
                                          n_tok_all=n_tok, row_off=0, carry=carry, colmajor=last)
        block_e, slot_src, slot_dst, slot_gate, n_rows = moe_tables(eid, gate, n_tok)
        Y = moe_experts(xm, block_e, slot_src, slot_dst, slot_gate, moe_w_gate[l].astype(BF16),
                        moe_w_up[l].astype(BF16), moe_w_down[l].astype(BF16), n_rows)
        x = moe_combine(x1, Y, m_lat, ln_g[l, 1], ln_b[l, 1], row_off=0)
        if not last:
            h_ctx = moe_combine(hc1, Y, m_ctx, ln_g[l, 1], ln_b[l, 1], row_off=B * S)
    return x
```

```python
import functools

import jax
import jax.numpy as jnp
from jax import lax
from jax.experimental import pallas as pl
from jax.experimental.pallas import tpu as pltpu

F32 = jnp.float32
BF16 = jnp.bfloat16
HIGHEST = lax.Precision.HIGHEST

D = 2048
DEPTH = 2
GRID_W = 64
CHUNK = 64
GLA_H, GLA_DK, GLA_DV, GLA_RANK, GLA_TAU = 4, 128, 256, 16, 16.0
GLA_QK, GLA_V = GLA_H * GLA_DK, GLA_H * GLA_DV
LRU_W, LRU_NB, LRU_C, CONV_W = 1024, 8, 8.0, 4
LRU_BW = LRU_W // LRU_NB
ML_H, ML_DK, ML_DV = 8, 128, 256
ML_QK, ML_V = ML_H * ML_DK, ML_H * ML_DV
M_INIT = -1e30
N_GROUPS, EPG, N_EXP, D_EXP = 4, 8, 32, 512
ALPHA = (2 * DEPTH) ** 0.25
EPS = 1e-5

LANES = 128
SUBLANES = 8
VMEM_LIMIT = 56 * 1024 * 1024

EV_NP = 5376
EV_TN = 768
EV_GATE_COL = 5120
OD_NP = 6400
OD_TN = 1280
OD_GATE_COL = 6144
MOE_BLK = 256


def _params(n_axes):
    return pltpu.CompilerParams(dimension_semantics=("arbitrary",) * n_axes, vmem_limit_bytes=VMEM_LIMIT)


def _sigmoid(x):
    return 1.0 / (1.0 + jnp.exp(-x))


def _silu(x):
    return x * _sigmoid(x)


def _log_sigmoid(x):
    return jnp.minimum(x, 0.0) - jnp.log1p(jnp.exp(-jnp.abs(x)))


def _gelu_tanh(x):
    return 0.5 * x * (1.0 + jnp.tanh(0.7978845608028654 * (x + 0.044715 * (x * x * x))))


def _bdot(a, b):
    return jnp.dot(a.astype(BF16), b.astype(BF16), preferred_element_type=F32)


def _bdot_nt(a, b):
    return lax.dot_general(a.astype(BF16), b.astype(BF16), (((1,), (1,)), ((), ())), preferred_element_type=F32)


def _bdot_tn(a, b):
    return lax.dot_general(a.astype(BF16), b.astype(BF16), (((0,), (0,)), ((), ())), preferred_element_type=F32)


def _split2(x):
    hi = x.astype(BF16)
    return hi, (x - hi.astype(F32)).astype(BF16)


def _split3(x):
    hi = x.astype(BF16)
    r = x - hi.astype(F32)
    mid = r.astype(BF16)
    return hi, mid, (r - mid.astype(F32)).astype(BF16)


def _mod_kernel(c_ref, w_ref, b_ref, o_ref):
    o_ref[...] = _bdot(_silu(c_ref[...]), w_ref[...]) + b_ref[...]


def cond_mod(cond8, w_mod, b_mod):
    nl, _, n6 = w_mod.shape
    tn = 1024
    return pl.pallas_call(
        _mod_kernel,
        grid=(nl, n6 // tn),
        in_specs=[
            pl.BlockSpec((SUBLANES, D), lambda l, j: (0, 0)),
            pl.BlockSpec((None, D, tn), lambda l, j: (l, 0, j)),
            pl.BlockSpec((None, 1, tn), lambda l, j: (l, 0, j)),
        ],
        out_specs=pl.BlockSpec((None, SUBLANES, tn), lambda l, j: (l, 0, j)),
        out_shape=jax.ShapeDtypeStruct((nl, SUBLANES, n6), F32),
        compiler_params=_params(2),
        name="cond_mod",
    )(cond8, w_mod, b_mod.reshape(nl, 1, n6))


def _inproj_kernel(x_ref, m_ref, w_ref, o_ref, u_ref, *, tr, nslab):
    @pl.when(pl.program_id(2) == 0)
    def _():
        shift = m_ref[0:1, :]
        scale = 1.0 + m_ref[1:2, :]
        for s in range(nslab):
            u_ref[s * tr:(s + 1) * tr, :] = (x_ref[:, s * D:(s + 1) * D] * scale + shift).astype(BF16)

    o_ref[...] = jnp.dot(u_ref[...], w_ref[...], preferred_element_type=F32)


def inproj(x3, mvec, w, *, tn, colmajor, tm=1024):
    B, T, _ = x3.shape
    Np = w.shape[1]
    if colmajor:
        R = T // GRID_W
        nslab = max(1, min(GRID_W, tm // R))
        tr, n_mt = R, GRID_W // nslab
        x_view = x3.reshape(B, R, GRID_W * D)
        x_spec = pl.BlockSpec((None, R, nslab * D), lambda b, i, j: (b, 0, i))
    else:
        tr, nslab = min(tm, T), 1
        n_mt = T // tr
        x_view = x3
        x_spec = pl.BlockSpec((None, tr, D), lambda b, i, j: (b, i, 0))
    tmm = tr * nslab
    return pl.pallas_call(
        functools.partial(_inproj_kernel, tr=tr, nslab=nslab),
        grid=(B, n_mt, Np // tn),
        in_specs=[
            x_spec,
            pl.BlockSpec((None, SUBLANES, D), lambda b, i, j: (b, 0, 0)),
            pl.BlockSpec((D, tn), lambda b, i, j: (0, j)),
        ],
        out_specs=pl.BlockSpec((None, tmm, tn), lambda b, i, j: (b, i, j)),
        out_shape=jax.ShapeDtypeStruct((B, T, Np), F32),
        scratch_shapes=[pltpu.VMEM((tmm, D), BF16)],
        compiler_params=_params(3),
        name="inproj",
    )(x_view, mvec, w)


def _gla_kernel(qf, kf, vf, gf, qb, kb, vb, gb, wa_ref, ba_ref, s0_ref, of_ref, ob_ref, sfin_ref, st_ref, *, tb, ns):
    s = pl.program_id(1)

    @pl.when(s == 0)
    def _():
        st_ref[...] = s0_ref[...]

    L = CHUNK
    row = lax.broadcasted_iota(jnp.int32, (L, L), 0)
    col = lax.broadcasted_iota(jnp.int32, (L, L), 1)
    row3 = lax.broadcasted_iota(jnp.int32, (L, 3 * L), 0)
    col3 = lax.rem(lax.broadcasted_iota(jnp.int32, (L, 3 * L), 1), L)
    scale = GLA_DK ** -0.5
    for d, (q_ref, k_ref, v_ref, g_ref, o_ref) in enumerate(((qf, kf, vf, gf, of_ref), (qb, kb, vb, gb, ob_ref))):
        mask = (row >= col) if d == 0 else (row <= col)
        m3 = jnp.where((row3 >= col3) if d == 0 else (row3 <= col3), 1.0, 0.0).astype(BF16)
        x_hi, x_lo = _split2(g_ref[...])
        gpre = jnp.dot(jnp.concatenate([x_hi, x_hi, x_lo], axis=1), wa_ref[d], preferred_element_type=F32)
        glog = _log_sigmoid(gpre + ba_ref[d]) * (1.0 / GLA_TAU)
        nchunk = tb // L
        for c in (range(nchunk) if d == 0 else range(nchunk - 1, -1, -1)):
            r0 = c * L
            g1, g2, g3 = _split3(glog[r0:r0 + L, :])
            b_all = jnp.dot(m3, jnp.concatenate([g1, g2, g3], axis=0), preferred_element_type=F32)
            for h in range(GLA_H):
                q = q_ref[r0:r0 + L, h * GLA_DK:(h + 1) * GLA_DK]
                k = k_ref[r0:r0 + L, h * GLA_DK:(h + 1) * GLA_DK]
                v = v_ref[r0:r0 + L, h * GLA_DV:(h + 1) * GLA_DV].astype(BF16)
                b = b_all[:, h * GLA_DK:(h + 1) * GLA_DK]
                bl = b[L - 1:L, :] if d == 0 else b[0:1, :]
                qd = (q * (jnp.exp(b) * scale)).astype(BF16)
                ki = k * jnp.exp(-b)
                ke = k * jnp.exp(bl - b)
                att = jnp.where(mask, _bdot_nt(qd, ki), 0.0)
                st = st_ref[d, h]
                o = _bdot(att, v) + _bdot_nt(qd, st)
                o_ref[r0:r0 + L, h * GLA_DV:(h + 1) * GLA_DV] = o
                st_ref[d, h] = st * jnp.exp(bl) + _bdot_tn(v, ke)

    @pl.when(s == ns - 1)
    def _():
        sfin_ref[...] = st_ref[...]


def gla_scan(P, wa, ba, s0, *, tb=256):
    B, T, _ = P.shape
    tb = min(tb, T)
    ns = T // tb
    fwd = lambda b, s: s
    bwd = lambda b, s: ns - 1 - s

    def specs(pos):
        return [
            pl.BlockSpec((None, tb, GLA_QK), lambda b, s: (b, pos(b, s), 0)),
            pl.BlockSpec((None, tb, GLA_QK), lambda b, s: (b, pos(b, s), 1)),
            pl.BlockSpec((None, tb, GLA_V), lambda b, s: (b, pos(b, s), 1)),
            pl.BlockSpec((None, tb, LANES), lambda b, s: (b, pos(b, s), EV_GATE_COL // LANES)),
        ]

    st_shape = (2, GLA_H, GLA_DV, GLA_DK)
    st_spec = pl.BlockSpec((None,) + st_shape, lambda b, s: (b, 0, 0, 0, 0))
    return pl.pallas_call(
        functools.partial(_gla_kernel, tb=tb, ns=ns),
        grid=(B, ns),
        in_specs=specs(fwd) + specs(bwd) + [
            pl.BlockSpec((2, 3 * LANES, GLA_QK), lambda b, s: (0, 0, 0)),
            pl.BlockSpec((2, 1, GLA_QK), lambda b, s: (0, 0, 0)),
            st_spec,
        ],
        out_specs=[
            pl.BlockSpec((None, tb, GLA_V), lambda b, s: (b, s, 0)),
            pl.BlockSpec((None, tb, GLA_V), lambda b, s: (b, ns - 1 - s, 0)),
            st_spec,
        ],
        out_shape=[
            jax.ShapeDtypeStruct((B, T, GLA_V), F32),
            jax.ShapeDtypeStruct((B, T, GLA_V), F32),
            jax.ShapeDtypeStruct((B,) + st_shape, F32),
        ],
        scratch_shapes=[pltpu.VMEM(st_shape, F32)],
        compiler_params=_params(2),
        name="gla_scan",
    )(P, P, P, P, P, P, P, P, wa, ba, s0)


def _lru_kernel(xf, pf, nf, xb, pb, nb_, cw_ref, cb_ref, w_ref, bias_ref, lam_ref, h0_ref,
                hf_ref, hb_ref, hfin_ref, a_s, u_s, h_s, *, tm, nt):
    i = pl.program_id(1)

    @pl.when(i == 0)
    def _():
        h_s[...] = h0_ref[...]

    cw = cw_ref[...]
    off = SUBLANES - CONV_W // 2
    for d, (x_ref, p_ref, n_ref) in enumerate(((xf, pf, nf), (xb, pb, nb_))):
        ti = i if d == 0 else nt - 1 - i
        prev = jnp.where(ti > 0, p_ref[...], 0.0)
        nxt = jnp.where(ti < nt - 1, n_ref[...], 0.0)
        ext = jnp.concatenate([prev, x_ref[...], nxt], axis=0)
        xc = cb_ref[...] + (ext[off:off + tm] * cw[0:1] + ext[off + 1:off + 1 + tm] * cw[1:2]
                            + ext[off + 2:off + 2 + tm] * cw[2:3] + ext[off + 3:off + 3 + tm] * cw[3:4])
        for n in range(LRU_NB):
            cs = slice(n * LRU_BW, (n + 1) * LRU_BW)
            xn = xc[:, cs]
            z = _bdot(xn, w_ref[d, n]) + bias_ref[d, n]
            r = _sigmoid(z[:, :LRU_BW])
            ig = _sigmoid(z[:, LRU_BW:])
            log_a = (LRU_C * r) * _log_sigmoid(lam_ref[d:d + 1, cs])
            a_s[d, :, cs] = jnp.exp(log_a)
            u_s[d, :, cs] = jnp.sqrt(1.0 - jnp.exp(2.0 * log_a)) * (ig * xn)

    sub = lax.broadcasted_iota(jnp.int32, (SUBLANES, LRU_W), 0)

    def group(g, carry):
        hf, hb = carry
        bf = pl.multiple_of(g * SUBLANES, SUBLANES)
        bb = pl.multiple_of(tm - SUBLANES - g * SUBLANES, SUBLANES)
        a_f, u_f = a_s[0, pl.ds(bf, SUBLANES), :], u_s[0, pl.ds(bf, SUBLANES), :]
        a_b, u_b = a_s[1, pl.ds(bb, SUBLANES), :], u_s[1, pl.ds(bb, SUBLANES), :]
        for j in range(SUBLANES):
            hf = jnp.where(sub == j, a_f * pltpu.roll(hf, 1, axis=0) + u_f, hf)
            jb = SUBLANES - 1 - j
            hb = jnp.where(sub == jb, a_b * pltpu.roll(hb, SUBLANES - 1, axis=0) + u_b, hb)
        hf_ref[pl.ds(bf, SUBLANES), :] = hf
        hb_ref[pl.ds(bb, SUBLANES), :] = hb
        return hf, hb

    hf, hb = lax.fori_loop(0, tm // SUBLANES, group, (h_s[0], h_s[1]))
    h_s[0] = hf
    h_s[1] = hb

    @pl.when(i == nt - 1)
    def _():
        hfin_ref[...] = h_s[...]


def lru_mix(P, conv_w, conv_b, wdir, bdir, lam, h0, *, tm=512):
    B, T, _ = P.shape
    tm = min(tm, T)
    nt = T // tm
    xcol = 3
    r8 = tm // SUBLANES
    n8 = T // SUBLANES
    fwd = lambda i: i
    bwd = lambda i: nt - 1 - i

    def xspecs(pos):
        return [
            pl.BlockSpec((None, tm, LRU_W), lambda b, i: (b, pos(i), xcol)),
            pl.BlockSpec((None, SUBLANES, LRU_W), lambda b, i: (b, jnp.maximum(pos(i) * r8 - 1, 0), xcol)),
            pl.BlockSpec((None, SUBLANES, LRU_W), lambda b, i: (b, jnp.minimum((pos(i) + 1) * r8, n8 - 1), xcol)),
        ]

    hspec = pl.BlockSpec((None, 2, SUBLANES, LRU_W), lambda b, i: (b, 0, 0, 0))
    seq = jax.ShapeDtypeStruct((B, T, LRU_W), F32)
    return pl.pallas_call(
        functools.partial(_lru_kernel, tm=tm, nt=nt),
        grid=(B, nt),
        in_specs=xspecs(fwd) + xspecs(bwd) + [
            _const_spec((CONV_W, LRU_W)), _const_spec((1, LRU_W)),
            _const_spec((2, LRU_NB, LRU_BW, 2 * LRU_BW)), _const_spec((2, LRU_NB, 1, 2 * LRU_BW)),
            _const_spec((2, LRU_W)), hspec,
        ],
        out_specs=[
            pl.BlockSpec((None, tm, LRU_W), lambda b, i: (b, i, 0)),
            pl.BlockSpec((None, tm, LRU_W), lambda b, i: (b, nt - 1 - i, 0)),
            hspec,
        ],
        out_shape=[seq, seq, jax.ShapeDtypeStruct((B, 2, SUBLANES, LRU_W), F32)],
        scratch_shapes=[pltpu.VMEM((2, tm, LRU_W), F32), pltpu.VMEM((2, tm, LRU_W), F32),
                        pltpu.VMEM((2, SUBLANES, LRU_W), F32)],
        compiler_params=_params(2),
        name="lru_mix",
    )(P, P, P, P, P, P, conv_w, conv_b.reshape(1, LRU_W), wdir, bdir, lam, h0)


ML_AUG = ML_DV + LANES
ML_L = 256


def _mlstm_kernel(qf, kf, vf, gf, qb, kb, vb, gb, bg_ref, c0_ref, m0_ref,
                  hf_ref, hb_ref, cfin_ref, mfin_ref, c_ref, m_ref, *, tb, ns):
    s = pl.program_id(2)

    @pl.when(s == 0)
    def _():
        c_ref[...] = c0_ref[...]
        m_ref[...] = m0_ref[...]

    L = tb
    row = lax.broadcasted_iota(jnp.int32, (L, L), 0)
    col = lax.broadcasted_iota(jnp.int32, (L, L), 1)
    sub = lax.broadcasted_iota(jnp.int32, (SUBLANES, L), 0)
    lane = lax.broadcasted_iota(jnp.int32, (SUBLANES, L), 1)
    is_f = (sub == 1) | (sub == 3)
    ones_col = (lax.broadcasted_iota(jnp.int32, (L, LANES), 1) == 0).astype(BF16)
    kscale = ML_DK ** -0.5
    pre = None
    for d, (q_ref, k_ref, v_ref, g_ref, h_ref) in enumerate(((qf, kf, vf, gf, hf_ref), (qb, kb, vb, gb, hb_ref))):
        mask = (row >= col) if d == 0 else (row <= col)
        ri, rf = 2 * d, 2 * d + 1
        pre = g_ref[0] + bg_ref[...]
        xg = jnp.where(is_f, _log_sigmoid(pre), pre)
        cs = xg
        k_ = 1
        while k_ < L:
            if d == 0:
                cs = cs + jnp.where(lane >= k_, pltpu.roll(cs, k_, axis=1), 0.0)
            else:
                cs = cs + jnp.where(lane < L - k_, pltpu.roll(cs, L - k_, axis=1), 0.0)
            k_ *= 2
        b_row, i_row = cs[rf:rf + 1, :], xg[ri:ri + 1, :]
        b_last = b_row[:, L - 1:L] if d == 0 else b_row[:, 0:1]
        rows2 = jnp.where(sub == 0, jnp.broadcast_to(b_row, (SUBLANES, L)),
                          jnp.where(sub == 1, jnp.broadcast_to(i_row, (SUBLANES, L)), 0.0))
        cols = jnp.concatenate([rows2, jnp.zeros((LANES - SUBLANES, L), F32)], axis=0).T
        b_col, i_col = cols[:, 0:1], cols[:, 1:2]
        dmat = jnp.where(mask, b_col + (i_row - b_row), -jnp.inf)
        m_loc = jnp.max(dmat, axis=1, keepdims=True)
        q = q_ref[...].astype(BF16)
        k = k_ref[...] * kscale
        vaug = jnp.concatenate([v_ref[...].astype(BF16), ones_col], axis=1)
        sc = _bdot_nt(q, k) * jnp.exp(dmat - m_loc)
        n_loc = _bdot(sc, vaug)
        g_row = b_last - b_row + i_row
        m_k = jnp.max(g_row, axis=1, keepdims=True)
        m = m_ref[d][0:1, 0:1]
        caug = c_ref[d]
        inter = b_col + m
        m_t = jnp.maximum(inter, m_loc)
        acc = jnp.exp(m_loc - m_t) * n_loc + jnp.exp(inter - m_t) * _bdot(q, caug)
        den = acc[:, ML_DV:ML_DV + 1]
        h_ref[...] = acc[:, :ML_DV] / jnp.maximum(jnp.abs(den), jnp.exp(-m_t))
        m_new = jnp.maximum(b_last + m, m_k)
        wk = jnp.exp(b_last - b_col + i_col - m_new)
        c_ref[d] = jnp.exp(b_last + m - m_new) * caug + _bdot_tn(wk * k, vaug)
        m_ref[d] = jnp.broadcast_to(m_new, (SUBLANES, LANES))

    @pl.when(s == ns - 1)
    def _():
        cfin_ref[...] = c_ref[...]
        mfin_ref[...] = m_ref[...]


def mlstm_scan(P, gt, bgate, c0, m0):
    B, T, _ = P.shape
    tb = ML_L
    ns = T // tb
    ncb = 1
    fwd = lambda s: s
    bwd = lambda s: ns - 1 - s

    def specs(pos):
        return [
            pl.BlockSpec((None, tb, ML_DK), lambda b, h, s: (b, pos(s), h)),
            pl.BlockSpec((None, tb, ML_DK), lambda b, h, s: (b, pos(s), ML_H + h)),
            pl.BlockSpec((None, tb, ML_DV), lambda b, h, s: (b, pos(s), ML_H + h)),
            pl.BlockSpec((None, None, ncb, SUBLANES, tb), lambda b, h, s: (b, h, pos(s), 0, 0)),
        ]

    cshape = (2, ML_DK, ML_AUG)
    mshape = (2, SUBLANES, LANES)
    cspec = pl.BlockSpec((None, 2, None, ML_DK, ML_AUG), lambda b, h, s: (b, 0, h, 0, 0))
    mspec = pl.BlockSpec((None, 2, None, SUBLANES, LANES), lambda b, h, s: (b, 0, h, 0, 0))
    return pl.pallas_call(
        functools.partial(_mlstm_kernel, tb=tb, ns=ns),
        grid=(B, ML_H, ns),
        in_specs=specs(fwd) + specs(bwd) + [
            pl.BlockSpec((None, SUBLANES, 1), lambda b, h, s: (h, 0, 0)),
            cspec, mspec,
        ],
        out_specs=[
            pl.BlockSpec((None, tb, ML_DV), lambda b, h, s: (b, s, h)),
            pl.BlockSpec((None, tb, ML_DV), lambda b, h, s: (b, ns - 1 - s, h)),
            cspec, mspec,
        ],
        out_shape=[
            jax.ShapeDtypeStruct((B, T, ML_V), F32),
            jax.ShapeDtypeStruct((B, T, ML_V), F32),
            jax.ShapeDtypeStruct((B, 2, ML_H, ML_DK, ML_AUG), F32),
            jax.ShapeDtypeStruct((B, 2, ML_H, SUBLANES, LANES), F32),
        ],
        scratch_shapes=[pltpu.VMEM(cshape, F32), pltpu.VMEM(mshape, F32)],
        compiler_params=_params(3),
        name="mlstm_scan",
    )(P, P, P, gt, P, P, P, gt, bgate, c0, m0)


def _layer_norm(z, g, b):
    mu = jnp.mean(z, axis=-1, keepdims=True)
    zc = z - mu
    var = jnp.mean(zc * zc, axis=-1, keepdims=True)
    return zc * lax.rsqrt(var + EPS) * g + b


def _head_rms(o, g, n_heads, width):
    parts = []
    for h in range(n_heads):
        oh = o[:, h * width:(h + 1) * width]
        parts.append(oh * lax.rsqrt(jnp.mean(oh * oh, axis=-1, keepdims=True) + EPS) * g)
    return jnp.concatenate(parts, axis=-1)


def _post(y, x_ref, m_ref, lng_ref, lnb_ref, rw_refs, x1_ref, xm_ref, eid_ref, gate_ref):
    wgh, wgl, weh, wel, bgr, ber = rw_refs
    x1 = _layer_norm(ALPHA * x_ref[...] + m_ref[2:3, :] * y, lng_ref[...], lnb_ref[...])
    x1_ref[...] = x1
    xm = x1 * (1.0 + m_ref[4:5, :]) + m_ref[3:4, :]
    xm_ref[...] = xm
    hi = xm.astype(BF16)
    lo = (xm - hi.astype(F32)).astype(BF16)

    def logits(wh, wl, bias):
        return (jnp.dot(hi, wh[...], preferred_element_type=F32) + jnp.dot(hi, wl[...], preferred_element_type=F32)
                + jnp.dot(lo, wh[...], preferred_element_type=F32)) + bias[...]

    gl = logits(wgh, wgl, bgr)
    el = logits(weh, wel, ber)
    lane = lax.broadcasted_iota(jnp.int32, gl.shape, 1)
    neg = -jnp.inf
    glm = jnp.where(lane < N_GROUPS, gl, neg)
    gmax = jnp.max(glm, axis=-1, keepdims=True)
    gidx = jnp.min(jnp.where(glm == gmax, lane, LANES), axis=-1, keepdims=True)
    p_g = 1.0 / jnp.sum(jnp.exp(glm - gmax), axis=-1, keepdims=True)
    in_group = (lax.shift_right_logical(lane, 3) == gidx) & (lane < N_EXP)
    elm = jnp.where(in_group, el, neg)
    m1 = jnp.max(elm, axis=-1, keepdims=True)
    i1 = jnp.min(jnp.where(elm == m1, lane, LANES), axis=-1, keepdims=True)
    elm2 = jnp.where(lane == i1, neg, elm)
    m2 = jnp.max(elm2, axis=-1, keepdims=True)
    i2 = jnp.min(jnp.where(elm2 == m2, lane, LANES), axis=-1, keepdims=True)
    e2 = jnp.exp(m2 - m1)
    g1 = p_g / (1.0 + e2)
    g2 = g1 * e2
    eid_ref[...] = jnp.where(lane == 0, i1, jnp.where(lane == 1, i2, 0))
    gate_ref[...] = jnp.where(lane == 0, g1, jnp.where(lane == 1, g2, 0.0))


def _out_even_kernel(of, ob, r, hf, hb, xg, x, m, gg, wout, lng, lnb, wgh, wgl, weh, wel, bgr, ber, *rest):
    x1_ref, xm_ref, eid_ref, gate_ref = rest[-4:]
    o = _head_rms(of[...] + ob[...], gg[...], GLA_H, GLA_DV) * _silu(r[...])
    y2 = (hf[...] + hb[...]) * _gelu_tanh(xg[...])
    mix = jnp.concatenate([o.astype(BF16), y2.astype(BF16)], axis=-1)
    y = jnp.dot(mix, wout[...], preferred_element_type=F32)
    _post(y, x, m, lng, lnb, (wgh, wgl, weh, wel, bgr, ber), x1_ref, xm_ref, eid_ref, gate_ref)


def _out_odd_kernel(hf, hb, og, x, m, gg, wout, lng, lnb, wgh, wgl, weh, wel, bgr, ber, *rest):
    x1_ref, xm_ref, eid_ref, gate_ref = rest[-4:]
    o = _head_rms(hf[...] + hb[...], gg[...], ML_H, ML_DV) * _sigmoid(og[...])
    y = jnp.dot(o.astype(BF16), wout[...], preferred_element_type=F32)
    _post(y, x, m, lng, lnb, (wgh, wgl, weh, wel, bgr, ber), x1_ref, xm_ref, eid_ref, gate_ref)


def _router_weights(w_group, b_group, w_expert, b_expert):
    def split(w, n):
        wp = jnp.zeros((D, LANES), F32).at[:, :n].set(w)
        hi = wp.astype(BF16)
        return hi, (wp - hi.astype(F32)).astype(BF16)

    def padb(b, n):
        return jnp.zeros((1, LANES), F32).at[0, :n].set(b)

    wgh, wgl = split(w_group, N_GROUPS)
    weh, wel = split(w_expert, N_EXP)
    return wgh, wgl, weh, wel, padb(b_group, N_GROUPS), padb(b_expert, N_EXP)


def _const_spec(shape):
    nd = len(shape)
    return pl.BlockSpec(shape, lambda *_: (0,) * nd)


def out_block(kind, acts, x, mvec, gg, wout, lng, lnb, rw, *, n_tok_all, row_off, carry=None, colmajor=False, tm=256):
    B, T, _ = x.shape
    tm = min(tm, T)
    if colmajor:
        R = tm = T // GRID_W
        nt = GRID_W
        tok = lambda width: pl.BlockSpec((None, R, width), lambda b, i: (b, 0, i))
        x_in = x.reshape(B, R, GRID_W * D)
        view = lambda a, width: a.reshape(B, R, GRID_W * width)
        assert row_off == 0 and n_tok_all == B * T
    else:
        nt = T // tm
        tok = lambda width: pl.BlockSpec((None, tm, width), lambda b, i: (b, i, 0))
        x_in = x
    act = lambda width, cb=0: pl.BlockSpec((None, tm, width), lambda b, i: (b, i, cb))
    if kind == "even":
        of, ob, P, hf, hb = acts
        a_in = [of, ob, P, hf, hb, P]
        a_specs = [act(GLA_V), act(GLA_V), act(GLA_V, 2), act(LRU_W), act(LRU_W), act(LRU_W, 4)]
        body = _out_even_kernel
    else:
        hf, hb, P = acts
        a_in = [hf, hb, P]
        a_specs = [act(ML_V), act(ML_V), act(ML_V, 2)]
        body = _out_odd_kernel
    w_in = [mvec, gg.reshape(1, -1), wout, lng.reshape(1, D), lnb.reshape(1, D)] + list(rw)
    w_specs = [pl.BlockSpec((None, SUBLANES, D), lambda b, i: (b, 0, 0))] + [_const_spec(a.shape) for a in w_in[1:]]
    if colmajor:
        tab = lambda width: pl.BlockSpec((None, R, width), lambda b, i: (b, 0, i))
        tab_shape = lambda width, dt: jax.ShapeDtypeStruct((B, R, GRID_W * width), dt)
    else:
        rb = row_off // tm
        tab = lambda width: pl.BlockSpec((tm, width), lambda b, i: (rb + b * nt + i, 0))
        tab_shape = lambda width, dt: jax.ShapeDtypeStruct((n_tok_all, width), dt)
    out_specs = [tok(D), tab(D), tab(LANES), tab(LANES)]
    out_shape = [jax.ShapeDtypeStruct(x_in.shape, F32), tab_shape(D, F32), tab_shape(LANES, jnp.int32), tab_shape(LANES, F32)]
    c_in, c_specs, aliases = [], [], {}
    if carry is None and n_tok_all != B * T:
        carry = (jnp.zeros((n_tok_all, D), F32), jnp.zeros((n_tok_all, LANES), jnp.int32), jnp.zeros((n_tok_all, LANES), F32))
    if carry is not None:
        c_in = list(carry)
        c_specs = [pl.BlockSpec(memory_space=pl.ANY)] * 3
        base = len(a_in) + 1 + len(w_in)
        aliases = {base: 1, base + 1: 2, base + 2: 3}
    x1, xm, eid, gate = pl.pallas_call(
        body,
        grid=(B, nt),
        in_specs=a_specs + [tok(D)] + w_specs + c_specs,
        out_specs=out_specs,
        out_shape=out_shape,
        input_output_aliases=aliases,
        compiler_params=_params(2),
        name="out_" + kind,
    )(*a_in, x_in, *w_in, *c_in)
    if colmajor:
        x1 = x1.reshape(B, T, D)
        xm, eid, gate = xm.reshape(B * T, D), eid.reshape(B * T, LANES), gate.reshape(B * T, LANES)
    return x1, xm, eid, gate


def _expert_kernel(be_ref, src_a, src_b, src_c, dst_p, dst_a, dst_b, g_a, g_b, wg_a, wu_a, wd_a, wg_b, wu_b, wd_b,
                   xm_hbm, y_hbm, x0, x1, y0, y1, gs, ss, *, nsteps, n_slots):
    del be_ref
    i = pl.program_id(0)
    blk = MOE_BLK

    def gather(idx_ref, xbuf, sem):
        for r in range(blk):
            pltpu.make_async_copy(xm_hbm.at[pl.ds(idx_ref[0, r], 1)], xbuf.at[pl.ds(r, 1)], sem).start()

    def scatter(idx_ref, ybuf, sem):
        for r in range(blk):
            pltpu.make_async_copy(ybuf.at[pl.ds(r, 1)], y_hbm.at[pl.ds(idx_ref[0, r], 1)], sem).start()

    def wait_gather(xbuf, sem):
        pltpu.make_async_copy(xm_hbm.at[pl.ds(0, blk)], xbuf, sem).wait()

    def wait_scatter(ybuf, sem):
        pltpu.make_async_copy(ybuf, y_hbm.at[pl.ds(0, blk)], sem).wait()

    def mlp(xbuf, ybuf, wg, wu, wd, g):
        xb = xbuf[...].astype(BF16)
        hg = jnp.dot(xb, wg[...], preferred_element_type=F32)
        hu = jnp.dot(xb, wu[...], preferred_element_type=F32)
        h = (_silu(hg) * hu).astype(BF16)
        ybuf[...] = jnp.dot(h, wd[...], preferred_element_type=F32) * g[...]

    @pl.when(i == 0)
    def _():
        y0[...] = jnp.zeros_like(y0)
        y1[...] = jnp.zeros_like(y1)
        pltpu.make_async_copy(y0, y_hbm.at[pl.ds(n_slots + blk, blk)], ss.at[0]).start()
        gather(src_a, x0, gs.at[0])

    wait_gather(x0, gs.at[0])
    wait_scatter(y0, ss.at[0])
    gather(src_b, x1, gs.at[1])
    scatter(dst_p, y1, ss.at[1])
    mlp(x0, y0, wg_a, wu_a, wd_a, g_a)
    wait_gather(x1, gs.at[1])
    wait_scatter(y1, ss.at[1])
    gather(src_c, x0, gs.at[0])
    scatter(dst_a, y0, ss.at[0])
    mlp(x1, y1, wg_b, wu_b, wd_b, g_b)

    @pl.when(i == nsteps - 1)
    def _():
        scatter(dst_b, y1, ss.at[1])
        wait_gather(x0, gs.at[0])
        wait_scatter(y0, ss.at[0])
        wait_scatter(y1, ss.at[1])


def moe_experts(xm, block_e, slot_src, slot_dst, slot_gate, wg, wu, wd):
    nb, blk = slot_src.shape
    nsteps = nb // 2
    n_slots = nb * blk
    src3 = slot_src.reshape(nb, 1, blk)
    spare = (n_slots + jnp.arange(blk, dtype=jnp.int32)).reshape(1, 1, blk)
    dst3 = jnp.concatenate([spare, slot_dst.reshape(nb, 1, blk)], axis=0)
    gate3 = slot_gate.reshape(nb, blk, 1)
    smem_blk = lambda f: pl.BlockSpec((None, 1, blk), f, memory_space=pltpu.SMEM)
    wspecs = lambda o: [
        pl.BlockSpec((None, D, D_EXP), lambda i, be: (be[2 * i + o], 0, 0)),
        pl.BlockSpec((None, D, D_EXP), lambda i, be: (be[2 * i + o], 0, 0)),
        pl.BlockSpec((None, D_EXP, D), lambda i, be: (be[2 * i + o], 0, 0)),
    ]
    grid_spec = pltpu.PrefetchScalarGridSpec(
        num_scalar_prefetch=1,
        grid=(nsteps,),
        in_specs=[
            smem_blk(lambda i, be: (2 * i, 0, 0)),
            smem_blk(lambda i, be: (2 * i + 1, 0, 0)),
            smem_blk(lambda i, be: (jnp.minimum(2 * i + 2, nb - 1), 0, 0)),
            smem_blk(lambda i, be: (2 * i, 0, 0)),
            smem_blk(lambda i, be: (2 * i + 1, 0, 0)),
            smem_blk(lambda i, be: (2 * i + 2, 0, 0)),
            pl.BlockSpec((None, blk, 1), lambda i, be: (2 * i, 0, 0)),
            pl.BlockSpec((None, blk, 1), lambda i, be: (2 * i + 1, 0, 0)),
        ] + wspecs(0) + wspecs(1) + [pl.BlockSpec(memory_space=pl.ANY)],
        out_specs=pl.BlockSpec(memory_space=pl.ANY),
        scratch_shapes=[
            pltpu.VMEM((blk, D), F32), pltpu.VMEM((blk, D), F32),
            pltpu.VMEM((blk, D), F32), pltpu.VMEM((blk, D), F32),
            pltpu.SemaphoreType.DMA((2,)),
            pltpu.SemaphoreType.DMA((2,)),
        ],
    )
    return pl.pallas_call(
        functools.partial(_expert_kernel, nsteps=nsteps, n_slots=n_slots),
        grid_spec=grid_spec,
        out_shape=jax.ShapeDtypeStruct((n_slots + 2 * blk, D), F32),
        compiler_params=_params(1),
        name="moe_experts",
    )(block_e, src3, src3, src3, dst3, dst3, dst3, gate3, gate3, wg, wu, wd, wg, wu, wd, xm)


def moe_tables(eid, gate, n_tok):
    blk = MOE_BLK
    n_assign = 2 * n_tok
    flat_e = eid[:, :2].reshape(-1)
    flat_g = gate[:, :2].reshape(-1)
    counts = jnp.sum((flat_e[:, None] == jnp.arange(N_EXP, dtype=jnp.int32)[None, :]).astype(jnp.int32), axis=0)
    padded = (counts + blk - 1) // blk * blk
    pad_end = jnp.cumsum(padded)
    pad_start = pad_end - padded
    cnt_start = jnp.cumsum(counts) - counts
    order = jnp.argsort(flat_e).astype(jnp.int32)
    nb = -(-(n_assign + N_EXP * (blk - 1)) // blk)
    nb += nb % 2
    bstart = jnp.arange(nb, dtype=jnp.int32) * blk
    block_e = jnp.minimum(jnp.sum((pad_end[None, :] <= bstart[:, None]).astype(jnp.int32), axis=1), N_EXP - 1)
    lane = jnp.arange(blk, dtype=jnp.int32)[None, :]
    j = (bstart - pad_start[block_e])[:, None] + lane
    cnt_b = counts[block_e][:, None]
    valid = j < cnt_b
    n_valid_before = cnt_start[block_e][:, None] + jnp.minimum(j, cnt_b)
    a = order[jnp.clip(n_valid_before, 0, n_assign - 1)]
    slot = bstart[:, None] + lane
    slot_src = jnp.where(valid, a // 2, 0).astype(jnp.int32)
    slot_dst = jnp.where(valid, (a % 2) * n_tok + a // 2, n_assign + slot - n_valid_before).astype(jnp.int32)
    slot_gate = jnp.where(valid, flat_g[a], 0.0)
    return block_e.astype(jnp.int32), slot_src, slot_dst, slot_gate


def _combine_kernel(x_ref, y0_ref, y1_ref, m_ref, g_ref, b_ref, o_ref):
    y = y0_ref[...] + y1_ref[...]
    o_ref[...] = _layer_norm(ALPHA * x_ref[...] + m_ref[5:6, :] * y, g_ref[...], b_ref[...])


def moe_combine(x1, Y, mvec, lng, lnb, *, n_tok, row_off, tm=512):
    B, T, _ = x1.shape
    tm = min(tm, T)
    nt = T // tm
    rb = row_off // tm
    kb = n_tok // tm
    return pl.pallas_call(
        _combine_kernel,
        grid=(B, nt),
        in_specs=[
            pl.BlockSpec((None, tm, D), lambda b, i: (b, i, 0)),
            pl.BlockSpec((tm, D), lambda b, i: (rb + b * nt + i, 0)),
            pl.BlockSpec((tm, D), lambda b, i: (kb + rb + b * nt + i, 0)),
            pl.BlockSpec((None, SUBLANES, D), lambda b, i: (b, 0, 0)),
            _const_spec((1, D)), _const_spec((1, D)),
        ],
        out_specs=pl.BlockSpec((None, tm, D), lambda b, i: (b, i, 0)),
        out_shape=jax.ShapeDtypeStruct((B, T, D), F32),
        compiler_params=_params(2),
        name="moe_combine",
    )(x1, Y, Y, mvec, lng.reshape(1, D), lnb.reshape(1, D))


def _even_w_in(w):
    g0 = GLA_QK * 2 + GLA_V * 2
    g1 = g0 + 2 * GLA_RANK
    pad = jnp.zeros((D, EV_NP - w.shape[1]), w.dtype)
    return jnp.concatenate([w[:, :g0], w[:, g1:], w[:, g0:g1], pad], axis=1).astype(BF16)


def _odd_w_in(w):
    return jnp.concatenate([w, jnp.zeros((D, OD_NP - w.shape[1]), w.dtype)], axis=1).astype(BF16)


def _lru_weights(w_r, b_r, w_i, b_i):
    wdir = jnp.concatenate([w_r, w_i], axis=-1).astype(BF16)
    blk = lambda b: b.reshape(2, LRU_NB, 1, LRU_BW)
    bdir = jnp.concatenate([blk(b_r), blk(b_i)], axis=-1)
    return wdir, bdir


def _gla_gate_weights(w_a2, b_a):
    wa = jnp.zeros((2, LANES, GLA_QK), F32)
    wa = wa.at[0, :GLA_RANK].set(w_a2[0]).at[1, GLA_RANK:2 * GLA_RANK].set(w_a2[1])
    w_hi, w_lo = _split2(wa)
    return jnp.concatenate([w_hi, w_lo, w_hi], axis=1), b_a.reshape(2, 1, GLA_QK)


def _mlstm_gate_table(P, T):
    B = P.shape[0]
    g = P[:, :, OD_GATE_COL:OD_GATE_COL + 4 * ML_H].reshape(B, T // ML_L, ML_L, 4, ML_H)
    g = jnp.transpose(g, (0, 4, 1, 3, 2))
    return jnp.concatenate([g, jnp.zeros_like(g)], axis=3)


def _mvec(mods_l, rows):
    m6 = jnp.stack([mods_l[r] for r in rows]).reshape(len(rows), 6, D)
    return jnp.concatenate([m6, jnp.zeros((len(rows), 2, D), F32)], axis=1)


def kernel(x, c, ctx, c_ctx, w_mod, b_mod, ln_g, ln_b, ev_w_in, gla_w_a2, gla_b_a, gla_norm, lru_conv_w, lru_conv_b, lru_w_r, lru_b_r, lru_w_i, lru_b_i, lru_lam, ev_w_out, od_w_in, mlstm_b_gate, mlstm_norm, od_w_out, moe_w_group, moe_b_group, moe_w_expert, moe_b_expert, moe_w_gate, moe_w_up, moe_w_down):
    B, S, _ = x.shape
    T_ctx = ctx.shape[1]
    cond8 = jnp.zeros((SUBLANES, D), F32).at[:B].set(c).at[B].set(c_ctx)
    mods = cond_mod(cond8, w_mod, b_mod)
    h_ctx = ctx
    for l in range(DEPTH):
        j = l // 2
        last = l == DEPTH - 1
        m_lat = _mvec(mods[l], list(range(B)))
        m_ctx = _mvec(mods[l], [B] * B)
        rw = _router_weights(moe_w_group[l], moe_b_group[l], moe_w_expert[l], moe_b_expert[l])
        n_tok = B * S if last else B * (S + T_ctx)
        if l % 2 == 0:
            w_in = _even_w_in(ev_w_in[j])
            wa, ba = _gla_gate_weights(gla_w_a2[j], gla_b_a[j])
            wdir, bdir = _lru_weights(lru_w_r[j], lru_b_r[j], lru_w_i[j], lru_b_i[j])
            w_out = ev_w_out[j].astype(BF16)

            def mix_even(xin, mv, s0, h0):
                P = inproj(xin, mv, w_in, tn=EV_TN, colmajor=False)
                of, ob, sfin = gla_scan(P, wa, ba, s0)
                hf, hb, hfin = lru_mix(P, lru_conv_w[j], lru_conv_b[j], wdir, bdir, lru_lam[j], h0)
                return (of, ob, P, hf, hb), sfin, hfin

            s0 = jnp.zeros((B, 2, GLA_H, GLA_DV, GLA_DK), F32)
            h0 = jnp.zeros((B, 2, SUBLANES, LRU_W), F32)
            acts_c, s_c, h_c = mix_even(h_ctx, m_ctx, s0, h0)
            acts_l, _, _ = mix_even(x, m_lat, s_c, h_c)
            carry = None
            if not last:
                hc1, *carry = out_block("even", acts_c, h_ctx, m_ctx, gla_norm[j], w_out, ln_g[l, 0], ln_b[l, 0], rw,
                                        n_tok_all=n_tok, row_off=B * S)
            x1, xm, eid, gate = out_block("even", acts_l, x, m_lat, gla_norm[j], w_out, ln_g[l, 0], ln_b[l, 0], rw,
                                          n_tok_all=n_tok, row_off=0, carry=carry)
        else:
            w_in = _odd_w_in(od_w_in[j])
            w_out = od_w_out[j].astype(BF16)
            bgate = jnp.concatenate([mlstm_b_gate[j].T, jnp.zeros((ML_H, 4), F32)], axis=1).reshape(ML_H, SUBLANES, 1)

            def mix_odd(xin, mv, c0, m0, colmajor):
                P = inproj(xin, mv, w_in, tn=OD_TN, colmajor=colmajor)
                gt = _mlstm_gate_table(P, xin.shape[1])
                hf, hb, cfin, mfin = mlstm_scan(P, gt, bgate, c0, m0)
                return (hf, hb, P), cfin, mfin

            c0 = jnp.zeros((B, 2, ML_H, ML_DK, ML_AUG), F32)
            m0 = jnp.full((B, 2, ML_H, SUBLANES, LANES), M_INIT, F32)
            acts_c, c_c, m_c = mix_odd(h_ctx, m_ctx, c0, m0, False)
            acts_l, _, _ = mix_odd(x, m_lat, c_c, m_c, True)
            carry = None
            if not last:
                hc1, *carry = out_block("odd", acts_c, h_ctx, m_ctx, mlstm_norm[j], w_out, ln_g[l, 0], ln_b[l, 0], rw,
                                        n_tok_all=n_tok, row_off=B * S)
            x1, xm, eid, gate = out_block("odd", acts_l, x, m_lat, mlstm_norm[j], w_out, ln_g[l, 0], ln_b[l, 0], rw,
                                          n_tok_all=n_tok, row_off=0, carry=carry, colmajor=last)
        block_e, slot_src, slot_dst, slot_gate = moe_tables(eid, gate, n_tok)
        Y = moe_experts(xm, block_e, slot_src, slot_dst, slot_gate, moe_w_gate[l].astype(BF16),
                        moe_w_up[l].astype(BF16), moe_w_down[l].astype(BF16))
        x = moe_combine(x1, Y, m_lat, ln_g[l, 1], ln_b[l, 1], n_tok=n_tok, row_off=0)
        if not last:
            h_ctx = moe_combine(hc1, Y, m_ctx, ln_g[l, 1], ln_b[l, 1], n_tok=n_tok, row_off=B * S)
    return x
```

```python
import functools

import jax
import jax.numpy as jnp
from jax import lax
from jax.experimental import pallas as pl
from jax.experimental.pallas import tpu as pltpu

F32 = jnp.float32
BF16 = jnp.bfloat16
HIGHEST = lax.Precision.HIGHEST

D = 2048
DEPTH = 2
GRID_W = 64
CHUNK = 64
GLA_H, GLA_DK, GLA_DV, GLA_RANK, GLA_TAU = 4, 128, 256, 16, 16.0
GLA_QK, GLA_V = GLA_H * GLA_DK, GLA_H * GLA_DV
LRU_W, LRU_NB, LRU_C, CONV_W = 1024, 8, 8.0, 4
LRU_BW = LRU_W // LRU_NB
ML_H, ML_DK, ML_DV = 8, 128, 256
ML_QK, ML_V = ML_H * ML_DK, ML_H * ML_DV
M_INIT = -1e30
N_GROUPS, EPG, N_EXP, D_EXP = 4, 8, 32, 512
ALPHA = (2 * DEPTH) ** 0.25
EPS = 1e-5

LANES = 128
SUBLANES = 8
VMEM_LIMIT = 56 * 1024 * 1024

EV_NP = 5376
EV_TN = 1792
EV_GATE_COL = 5120
OD_NP = 6400
OD_TN = 1280
OD_GATE_COL = 6144
MOE_BLK = 256
TOK_ROWS = D // LANES
TOK_PITCH = 24


def _params(n_axes):
    return pltpu.CompilerParams(dimension_semantics=("arbitrary",) * n_axes, vmem_limit_bytes=VMEM_LIMIT)


def _sigmoid(x):
    return 1.0 / (1.0 + jnp.exp(-x))


def _sigmoid_tanh(x):
    return 0.5 * jnp.tanh(0.5 * x) + 0.5


def _silu(x):
    return x * _sigmoid(x)


def _log_sigmoid(x):
    return jnp.minimum(x, 0.0) - jnp.log1p(jnp.exp(-jnp.abs(x)))


def _gelu_tanh(x):
    return 0.5 * x * (1.0 + jnp.tanh(0.7978845608028654 * (x + 0.044715 * (x * x * x))))


def _bdot(a, b):
    return jnp.dot(a.astype(BF16), b.astype(BF16), preferred_element_type=F32)


def _bdot_nt(a, b):
    return lax.dot_general(a.astype(BF16), b.astype(BF16), (((1,), (1,)), ((), ())), preferred_element_type=F32)


def _bdot_tn(a, b):
    return lax.dot_general(a.astype(BF16), b.astype(BF16), (((0,), (0,)), ((), ())), preferred_element_type=F32)


def _split2(x):
    hi = x.astype(BF16)
    return hi, (x - hi.astype(F32)).astype(BF16)


def _split3(x):
    hi = x.astype(BF16)
    r = x - hi.astype(F32)
    mid = r.astype(BF16)
    return hi, mid, (r - mid.astype(F32)).astype(BF16)


def _mod_kernel(c_ref, w_ref, b_ref, o_ref):
    o_ref[...] = _bdot(_silu(c_ref[...]), w_ref[...]) + b_ref[...]


def cond_mod(cond8, w_mod, b_mod):
    nl, _, n6 = w_mod.shape
    tn = 1024
    return pl.pallas_call(
        _mod_kernel,
        grid=(nl, n6 // tn),
        in_specs=[
            pl.BlockSpec((SUBLANES, D), lambda l, j: (0, 0)),
            pl.BlockSpec((None, D, tn), lambda l, j: (l, 0, j)),
            pl.BlockSpec((None, 1, tn), lambda l, j: (l, 0, j)),
        ],
        out_specs=pl.BlockSpec((None, SUBLANES, tn), lambda l, j: (l, 0, j)),
        out_shape=jax.ShapeDtypeStruct((nl, SUBLANES, n6), F32),
        compiler_params=_params(2),
        name="cond_mod",
    )(cond8, w_mod, b_mod.reshape(nl, 1, n6))


def _inproj_kernel(x_ref, m_ref, w_ref, o_ref, u_ref):
    @pl.when(pl.program_id(2) == 0)
    def _():
        u_ref[...] = (x_ref[...] * (1.0 + m_ref[1:2, :]) + m_ref[0:1, :]).astype(BF16)

    o_ref[...] = jnp.dot(u_ref[...], w_ref[...], preferred_element_type=F32)


def inproj(x3, mvec, w, *, tn, tm=1024):
    B, T, _ = x3.shape
    Np = w.shape[1]
    tm = min(tm, T)
    return pl.pallas_call(
        _inproj_kernel,
        grid=(B, T // tm, Np // tn),
        in_specs=[
            pl.BlockSpec((None, tm, D), lambda b, i, j: (b, i, 0)),
            pl.BlockSpec((None, SUBLANES, D), lambda b, i, j: (b, 0, 0)),
            pl.BlockSpec((D, tn), lambda b, i, j: (0, j)),
        ],
        out_specs=pl.BlockSpec((None, tm, tn), lambda b, i, j: (b, i, j)),
        out_shape=jax.ShapeDtypeStruct((B, T, Np), F32),
        scratch_shapes=[pltpu.VMEM((tm, D), BF16)],
        compiler_params=_params(3),
        name="inproj",
    )(x3, mvec, w)


def _gla_kernel(qf, kf, vf, gf, qb, kb, vb, gb, wa_ref, ba_ref, s0_ref, of_ref, ob_ref, sfin_ref, st_ref, *, tb, ns):
    s = pl.program_id(1)

    @pl.when(s == 0)
    def _():
        st_ref[...] = s0_ref[...]

    L = CHUNK
    row = lax.broadcasted_iota(jnp.int32, (L, L), 0)
    col = lax.broadcasted_iota(jnp.int32, (L, L), 1)
    row3 = lax.broadcasted_iota(jnp.int32, (L, 3 * L), 0)
    col3 = lax.rem(lax.broadcasted_iota(jnp.int32, (L, 3 * L), 1), L)
    scale = GLA_DK ** -0.5
    for d, (q_ref, k_ref, v_ref, g_ref, o_ref) in enumerate(((qf, kf, vf, gf, of_ref), (qb, kb, vb, gb, ob_ref))):
        mask = (row >= col) if d == 0 else (row <= col)
        m3 = jnp.where((row3 >= col3) if d == 0 else (row3 <= col3), 1.0, 0.0).astype(BF16)
        x_hi, x_lo = _split2(g_ref[...])
        gpre = jnp.dot(jnp.concatenate([x_hi, x_hi, x_lo], axis=1), wa_ref[d], preferred_element_type=F32)
        glog = _log_sigmoid(gpre + ba_ref[d]) * (1.0 / GLA_TAU)
        nchunk = tb // L
        for c in (range(nchunk) if d == 0 else range(nchunk - 1, -1, -1)):
            r0 = c * L
            g1, g2, g3 = _split3(glog[r0:r0 + L, :])
            b_all = jnp.dot(m3, jnp.concatenate([g1, g2, g3], axis=0), preferred_element_type=F32)
            for h in range(GLA_H):
                q = q_ref[r0:r0 + L, h * GLA_DK:(h + 1) * GLA_DK]
                k = k_ref[r0:r0 + L, h * GLA_DK:(h + 1) * GLA_DK]
                v = v_ref[r0:r0 + L, h * GLA_DV:(h + 1) * GLA_DV].astype(BF16)
                b = b_all[:, h * GLA_DK:(h + 1) * GLA_DK]
                bl = b[L - 1:L, :] if d == 0 else b[0:1, :]
                qd = (q * (jnp.exp(b) * scale)).astype(BF16)
                ki = k * jnp.exp(-b)
                ke = k * jnp.exp(bl - b)
                att = jnp.where(mask, _bdot_nt(qd, ki), 0.0)
                st = st_ref[d, h]
                o = _bdot(att, v) + _bdot_nt(qd, st)
                o_ref[r0:r0 + L, h * GLA_DV:(h + 1) * GLA_DV] = o
                st_ref[d, h] = st * jnp.exp(bl) + _bdot_tn(v, ke)

    @pl.when(s == ns - 1)
    def _():
        sfin_ref[...] = st_ref[...]


def gla_scan(P, wa, ba, s0, *, tb=256):
    B, T, _ = P.shape
    tb = min(tb, T)
    ns = T // tb
    fwd = lambda b, s: s
    bwd = lambda b, s: ns - 1 - s

    def specs(pos):
        return [
            pl.BlockSpec((None, tb, GLA_QK), lambda b, s: (b, pos(b, s), 0)),
            pl.BlockSpec((None, tb, GLA_QK), lambda b, s: (b, pos(b, s), 1)),
            pl.BlockSpec((None, tb, GLA_V), lambda b, s: (b, pos(b, s), 1)),
            pl.BlockSpec((None, tb, LANES), lambda b, s: (b, pos(b, s), EV_GATE_COL // LANES)),
        ]

    st_shape = (2, GLA_H, GLA_DV, GLA_DK)
    st_spec = pl.BlockSpec((None,) + st_shape, lambda b, s: (b, 0, 0, 0, 0))
    return pl.pallas_call(
        functools.partial(_gla_kernel, tb=tb, ns=ns),
        grid=(B, ns),
        in_specs=specs(fwd) + specs(bwd) + [
            pl.BlockSpec((2, 3 * LANES, GLA_QK), lambda b, s: (0, 0, 0)),
            pl.BlockSpec((2, 1, GLA_QK), lambda b, s: (0, 0, 0)),
            st_spec,
        ],
        out_specs=[
            pl.BlockSpec((None, tb, GLA_V), lambda b, s: (b, s, 0)),
            pl.BlockSpec((None, tb, GLA_V), lambda b, s: (b, ns - 1 - s, 0)),
            st_spec,
        ],
        out_shape=[
            jax.ShapeDtypeStruct((B, T, GLA_V), F32),
            jax.ShapeDtypeStruct((B, T, GLA_V), F32),
            jax.ShapeDtypeStruct((B,) + st_shape, F32),
        ],
        scratch_shapes=[pltpu.VMEM(st_shape, F32)],
        compiler_params=_params(2),
        name="gla_scan",
    )(P, P, P, P, P, P, P, P, wa, ba, s0)


def _lru_kernel(xf, pf, nf, xb, pb, nb_, cw_ref, cb_ref, w_ref, bias_ref, lam_ref, h0_ref,
                hf_ref, hb_ref, hfin_ref, a_s, u_s, h_s, *, tm, nt):
    i = pl.program_id(1)

    @pl.when(i == 0)
    def _():
        h_s[...] = h0_ref[...]

    cw = cw_ref[...]
    off = SUBLANES - CONV_W // 2
    for d, (x_ref, p_ref, n_ref) in enumerate(((xf, pf, nf), (xb, pb, nb_))):
        ti = i if d == 0 else nt - 1 - i
        prev = jnp.where(ti > 0, p_ref[...], 0.0)
        nxt = jnp.where(ti < nt - 1, n_ref[...], 0.0)
        ext = jnp.concatenate([prev, x_ref[...], nxt], axis=0)
        xc = cb_ref[...] + (ext[off:off + tm] * cw[0:1] + ext[off + 1:off + 1 + tm] * cw[1:2]
                            + ext[off + 2:off + 2 + tm] * cw[2:3] + ext[off + 3:off + 3 + tm] * cw[3:4])
        for n in range(LRU_NB):
            cs = slice(n * LRU_BW, (n + 1) * LRU_BW)
            xn = xc[:, cs]
            z = _bdot(xn, w_ref[d, n]) + bias_ref[d, n]
            r = _sigmoid_tanh(z[:, :LRU_BW])
            ig = _sigmoid_tanh(z[:, LRU_BW:])
            a = jnp.exp((LRU_C * r) * _log_sigmoid(lam_ref[d:d + 1, cs]))
            a_s[d, :, cs] = a
            u_s[d, :, cs] = jnp.sqrt(1.0 - a * a) * (ig * xn)

    sub = lax.broadcasted_iota(jnp.int32, (SUBLANES, LRU_W), 0)

    def group(g, carry):
        hf, hb = carry
        bf = pl.multiple_of(g * SUBLANES, SUBLANES)
        bb = pl.multiple_of(tm - SUBLANES - g * SUBLANES, SUBLANES)
        a_f, u_f = a_s[0, pl.ds(bf, SUBLANES), :], u_s[0, pl.ds(bf, SUBLANES), :]
        a_b, u_b = a_s[1, pl.ds(bb, SUBLANES), :], u_s[1, pl.ds(bb, SUBLANES), :]
        for j in range(SUBLANES):
            hf = jnp.where(sub == j, a_f * pltpu.roll(hf, 1, axis=0) + u_f, hf)
            jb = SUBLANES - 1 - j
            hb = jnp.where(sub == jb, a_b * pltpu.roll(hb, SUBLANES - 1, axis=0) + u_b, hb)
        hf_ref[pl.ds(bf, SUBLANES), :] = hf
        hb_ref[pl.ds(bb, SUBLANES), :] = hb
        return hf, hb

    hf, hb = lax.fori_loop(0, tm // SUBLANES, group, (h_s[0], h_s[1]))
    h_s[0] = hf
    h_s[1] = hb

    @pl.when(i == nt - 1)
    def _():
        hfin_ref[...] = h_s[...]


def lru_mix(P, conv_w, conv_b, wdir, bdir, lam, h0, *, tm=512):
    B, T, _ = P.shape
    tm = min(tm, T)
    nt = T // tm
    xcol = 3
    r8 = tm // SUBLANES
    n8 = T // SUBLANES
    fwd = lambda i: i
    bwd = lambda i: nt - 1 - i

    def xspecs(pos):
        return [
            pl.BlockSpec((None, tm, LRU_W), lambda b, i: (b, pos(i), xcol)),
            pl.BlockSpec((None, SUBLANES, LRU_W), lambda b, i: (b, jnp.maximum(pos(i) * r8 - 1, 0), xcol)),
            pl.BlockSpec((None, SUBLANES, LRU_W), lambda b, i: (b, jnp.minimum((pos(i) + 1) * r8, n8 - 1), xcol)),
        ]

    hspec = pl.BlockSpec((None, 2, SUBLANES, LRU_W), lambda b, i: (b, 0, 0, 0))
    seq = jax.ShapeDtypeStruct((B, T, LRU_W), F32)
    return pl.pallas_call(
        functools.partial(_lru_kernel, tm=tm, nt=nt),
        grid=(B, nt),
        in_specs=xspecs(fwd) + xspecs(bwd) + [
            _const_spec((CONV_W, LRU_W)), _const_spec((1, LRU_W)),
            _const_spec((2, LRU_NB, LRU_BW, 2 * LRU_BW)), _const_spec((2, LRU_NB, 1, 2 * LRU_BW)),
            _const_spec((2, LRU_W)), hspec,
        ],
        out_specs=[
            pl.BlockSpec((None, tm, LRU_W), lambda b, i: (b, i, 0)),
            pl.BlockSpec((None, tm, LRU_W), lambda b, i: (b, nt - 1 - i, 0)),
            hspec,
        ],
        out_shape=[seq, seq, jax.ShapeDtypeStruct((B, 2, SUBLANES, LRU_W), F32)],
        scratch_shapes=[pltpu.VMEM((2, tm, LRU_W), F32), pltpu.VMEM((2, tm, LRU_W), F32),
                        pltpu.VMEM((2, SUBLANES, LRU_W), F32)],
        compiler_params=_params(2),
        name="lru_mix",
    )(P, P, P, P, P, P, conv_w, conv_b.reshape(1, LRU_W), wdir, bdir, lam, h0)


ML_AUG = ML_DV + LANES
ML_L = 256


def _chunk_rows(ref, w):
    L, nwl, width = ref.shape
    flat = ref.reshape(L * nwl, width)
    return flat.at[pl.ds(w, L, stride=nwl), :] if nwl > 1 else flat


def _mlstm_kernel(qf, kf, vf0, vf1, gf, qb, kb, vb0, vb1, gb, bg_ref, c0_ref, m0_ref,
                  hf0_ref, hf1_ref, hb0_ref, hb1_ref, cfin_ref, mfin_ref, c_ref, m_ref, *, L, nwl, ns):
    s = pl.program_id(2)

    @pl.when(s == 0)
    def _():
        c_ref[...] = c0_ref[...]
        m_ref[...] = m0_ref[...]

    row = lax.broadcasted_iota(jnp.int32, (L, L), 0)
    col = lax.broadcasted_iota(jnp.int32, (L, L), 1)
    nr = nwl * SUBLANES
    sub = lax.rem(lax.broadcasted_iota(jnp.int32, (nr, L), 0), SUBLANES)
    lane = lax.broadcasted_iota(jnp.int32, (nr, L), 1)
    is_f = (sub == 1) | (sub == 3)
    ones_col = (lax.broadcasted_iota(jnp.int32, (L, LANES), 1) == 0).astype(BF16)
    kscale = ML_DK ** -0.5
    bias = jnp.concatenate([bg_ref[...]] * nwl, axis=0)

    chunk_rows = _chunk_rows

    def gate_vectors(g_ref, d):
        pre = g_ref[...].reshape(nr, L) + bias
        xg = jnp.where(is_f, _log_sigmoid(pre), pre)
        cs = xg
        k_ = 1
        while k_ < L:
            if d == 0:
                cs = cs + jnp.where(lane >= k_, pltpu.roll(cs, k_, axis=1), 0.0)
            else:
                cs = cs + jnp.where(lane < L - k_, pltpu.roll(cs, L - k_, axis=1), 0.0)
            k_ *= 2
        ri, rf = 2 * d, 2 * d + 1
        pad = jnp.zeros((LANES - nr, L), F32)
        cs_t = jnp.concatenate([cs, pad], axis=0).T
        xg_t = jnp.concatenate([xg, pad], axis=0).T
        out = []
        for w in range(nwl):
            jb, ji = SUBLANES * w + rf, SUBLANES * w + ri
            out.append((cs[jb:jb + 1, :], xg[ji:ji + 1, :], cs_t[:, jb:jb + 1], xg_t[:, ji:ji + 1]))
        return out

    vec = (gate_vectors(gf, 0), gate_vectors(gb, 1))
    work = [(0, w, qf, kf, (vf0, vf1), (hf0_ref, hf1_ref)) for w in range(nwl)]
    work_b = [(1, w, qb, kb, (vb0, vb1), (hb0_ref, hb1_ref)) for w in range(nwl - 1, -1, -1)]
    for pair in zip(work, work_b):
      for d, w, q_ref, k_ref, v_refs, h_refs in pair:
        mask = (row >= col) if d == 0 else (row <= col)
        b_row, i_row, b_col, i_col = vec[d][w]
        b_last = b_row[:, L - 1:L] if d == 0 else b_row[:, 0:1]
        dmat = jnp.where(mask, b_col + (i_row - b_row), -jnp.inf)
        m_loc = jnp.max(dmat, axis=1, keepdims=True)
        q = chunk_rows(q_ref, w)[...].astype(BF16)
        k = chunk_rows(k_ref, w)[...] * kscale
        vaug = jnp.concatenate([chunk_rows(v_refs[0], w)[...].astype(BF16), chunk_rows(v_refs[1], w)[...].astype(BF16),
                                ones_col], axis=1)
        sc = _bdot_nt(q, k) * jnp.exp(dmat - m_loc)
        n_loc = _bdot(sc, vaug)
        g_row = b_last - b_row + i_row
        m_k = jnp.max(g_row, axis=1, keepdims=True)
        kv_loc = _bdot_tn(jnp.exp(b_last - b_col + i_col - m_k) * k, vaug)
        m = m_ref[d][0:1, 0:1]
        caug = c_ref[d]
        inter = b_col + m
        m_t = jnp.maximum(inter, m_loc)
        acc = jnp.exp(m_loc - m_t) * n_loc + jnp.exp(inter - m_t) * _bdot(q, caug)
        den = acc[:, ML_DV:ML_DV + 1]
        h = acc[:, :ML_DV] / jnp.maximum(jnp.abs(den), jnp.exp(-m_t))
        chunk_rows(h_refs[0], w)[...] = h[:, :LANES]
        chunk_rows(h_refs[1], w)[...] = h[:, LANES:]
        m_new = jnp.maximum(b_last + m, m_k)
        c_ref[d] = jnp.exp(b_last + m - m_new) * caug + jnp.exp(m_k - m_new) * kv_loc
        m_ref[d] = jnp.broadcast_to(m_new, (SUBLANES, LANES))

    @pl.when(s == ns - 1)
    def _():
        cfin_ref[...] = c_ref[...]
        mfin_ref[...] = m_ref[...]


def mlstm_scan(P, gt, bgate, c0, m0, *, n_col):
    B, T, Np = P.shape
    L = T // n_col
    nwl = min(n_col, SUBLANES)
    ns = n_col // nwl
    P5 = P.reshape(B, L, ns, nwl, Np)
    fwd = lambda s: s
    bwd = lambda s: ns - 1 - s

    def specs(pos):
        return [
            pl.BlockSpec((None, L, None, nwl, ML_DK), lambda b, h, s: (b, 0, pos(s), 0, h)),
            pl.BlockSpec((None, L, None, nwl, ML_DK), lambda b, h, s: (b, 0, pos(s), 0, ML_H + h)),
            pl.BlockSpec((None, L, None, nwl, LANES), lambda b, h, s: (b, 0, pos(s), 0, 2 * ML_H + 2 * h)),
            pl.BlockSpec((None, L, None, nwl, LANES), lambda b, h, s: (b, 0, pos(s), 0, 2 * ML_H + 2 * h + 1)),
            pl.BlockSpec((None, None, nwl, SUBLANES, L), lambda b, h, s: (b, h, pos(s), 0, 0)),
        ]

    cshape = (2, ML_DK, ML_AUG)
    mshape = (2, SUBLANES, LANES)
    cspec = pl.BlockSpec((None, 2, None, ML_DK, ML_AUG), lambda b, h, s: (b, 0, h, 0, 0))
    mspec = pl.BlockSpec((None, 2, None, SUBLANES, LANES), lambda b, h, s: (b, 0, h, 0, 0))
    seq = jax.ShapeDtypeStruct((B, L, ns, nwl, ML_H * LANES), F32)
    ospec = lambda pos: pl.BlockSpec((None, L, None, nwl, LANES), lambda b, h, s: (b, 0, pos(s), 0, h))
    hf0, hf1, hb0, hb1, cfin, mfin = pl.pallas_call(
        functools.partial(_mlstm_kernel, L=L, nwl=nwl, ns=ns),
        grid=(B, ML_H, ns),
        in_specs=specs(fwd) + specs(bwd) + [
            pl.BlockSpec((None, SUBLANES, 1), lambda b, h, s: (h, 0, 0)),
            cspec, mspec,
        ],
        out_specs=[ospec(fwd), ospec(fwd), ospec(bwd), ospec(bwd), cspec, mspec],
        out_shape=[
            seq, seq, seq, seq,
            jax.ShapeDtypeStruct((B, 2, ML_H, ML_DK, ML_AUG), F32),
            jax.ShapeDtypeStruct((B, 2, ML_H, SUBLANES, LANES), F32),
        ],
        scratch_shapes=[pltpu.VMEM(cshape, F32), pltpu.VMEM(mshape, F32)],
        compiler_params=_params(3),
        name="mlstm_scan",
    )(P5, P5, P5, P5, gt, P5, P5, P5, P5, gt, bgate, c0, m0)
    flat = lambda a: a.reshape(B, T, ML_H * LANES)
    return (flat(hf0), flat(hf1), flat(hb0), flat(hb1)), cfin, mfin


def _layer_norm(z, g, b):
    mu = jnp.mean(z, axis=-1, keepdims=True)
    zc = z - mu
    var = jnp.mean(zc * zc, axis=-1, keepdims=True)
    return zc * lax.rsqrt(var + EPS) * g + b


def _head_rms(o, g, n_heads, width):
    parts = []
    for h in range(n_heads):
        oh = o[:, h * width:(h + 1) * width]
        parts.append(oh * lax.rsqrt(jnp.mean(oh * oh, axis=-1, keepdims=True) + EPS) * g)
    return jnp.concatenate(parts, axis=-1)


def _post(y, x_ref, m_ref, lng_ref, lnb_ref, rw_refs, x1_ref, xm_ref, eid_ref, gate_ref):
    wgh, wgl, weh, wel, bgr, ber = rw_refs
    x1 = _layer_norm(ALPHA * x_ref[...] + m_ref[2:3, :] * y, lng_ref[...], lnb_ref[...])
    x1_ref[...] = x1
    xm = x1 * (1.0 + m_ref[4:5, :]) + m_ref[3:4, :]
    tm = xm.shape[0]
    for j in range(TOK_ROWS):
        xm_ref[pl.ds(j, tm, stride=TOK_ROWS), :] = xm[:, j * LANES:(j + 1) * LANES]
    hi = xm.astype(BF16)
    lo = (xm - hi.astype(F32)).astype(BF16)

    def logits(wh, wl, bias):
        return (jnp.dot(hi, wh[...], preferred_element_type=F32) + jnp.dot(hi, wl[...], preferred_element_type=F32)
                + jnp.dot(lo, wh[...], preferred_element_type=F32)) + bias[...]

    gl = logits(wgh, wgl, bgr)
    el = logits(weh, wel, ber)
    lane = lax.broadcasted_iota(jnp.int32, gl.shape, 1)
    neg = -jnp.inf
    glm = jnp.where(lane < N_GROUPS, gl, neg)
    gmax = jnp.max(glm, axis=-1, keepdims=True)
    gidx = jnp.min(jnp.where(glm == gmax, lane, LANES), axis=-1, keepdims=True)
    p_g = 1.0 / jnp.sum(jnp.exp(glm - gmax), axis=-1, keepdims=True)
    in_group = (lax.shift_right_logical(lane, 3) == gidx) & (lane < N_EXP)
    elm = jnp.where(in_group, el, neg)
    m1 = jnp.max(elm, axis=-1, keepdims=True)
    i1 = jnp.min(jnp.where(elm == m1, lane, LANES), axis=-1, keepdims=True)
    elm2 = jnp.where(lane == i1, neg, elm)
    m2 = jnp.max(elm2, axis=-1, keepdims=True)
    i2 = jnp.min(jnp.where(elm2 == m2, lane, LANES), axis=-1, keepdims=True)
    e2 = jnp.exp(m2 - m1)
    g1 = p_g / (1.0 + e2)
    g2 = g1 * e2
    eid_ref[...] = jnp.where(lane == 0, i1, jnp.where(lane == 1, i2, 0))
    gate_ref[...] = jnp.where(lane == 0, g1, jnp.where(lane == 1, g2, 0.0))


def _out_even_kernel(of, ob, r, hf, hb, xg, x, m, gg, wout, lng, lnb, wgh, wgl, weh, wel, bgr, ber, *rest):
    x1_ref, xm_ref, eid_ref, gate_ref = rest[-4:]
    o = _head_rms(of[...] + ob[...], gg[...], GLA_H, GLA_DV) * _silu(r[...])
    y2 = (hf[...] + hb[...]) * _gelu_tanh(xg[...])
    mix = jnp.concatenate([o.astype(BF16), y2.astype(BF16)], axis=-1)
    y = jnp.dot(mix, wout[...], preferred_element_type=F32)
    _post(y, x, m, lng, lnb, (wgh, wgl, weh, wel, bgr, ber), x1_ref, xm_ref, eid_ref, gate_ref)


def _out_odd_kernel(hf0, hf1, hb0, hb1, og, x, m, gg, wout, lng, lnb, wgh, wgl, weh, wel, bgr, ber, *rest):
    x1_ref, xm_ref, eid_ref, gate_ref = rest[-4:]
    lo = hf0[...] + hb0[...]
    hi = hf1[...] + hb1[...]
    hsum = jnp.concatenate([half[:, h * LANES:(h + 1) * LANES] for h in range(ML_H) for half in (lo, hi)], axis=-1)
    o = _head_rms(hsum, gg[...], ML_H, ML_DV) * _sigmoid(og[...])
    y = jnp.dot(o.astype(BF16), wout[...], preferred_element_type=F32)
    _post(y, x, m, lng, lnb, (wgh, wgl, weh, wel, bgr, ber), x1_ref, xm_ref, eid_ref, gate_ref)


def _router_weights(w_group, b_group, w_expert, b_expert):
    def split(w, n):
        wp = jnp.zeros((D, LANES), F32).at[:, :n].set(w)
        hi = wp.astype(BF16)
        return hi, (wp - hi.astype(F32)).astype(BF16)

    def padb(b, n):
        return jnp.zeros((1, LANES), F32).at[0, :n].set(b)

    wgh, wgl = split(w_group, N_GROUPS)
    weh, wel = split(w_expert, N_EXP)
    return wgh, wgl, weh, wel, padb(b_group, N_GROUPS), padb(b_expert, N_EXP)


def _const_spec(shape):
    nd = len(shape)
    return pl.BlockSpec(shape, lambda *_: (0,) * nd)


def out_block(kind, acts, x, mvec, gg, wout, lng, lnb, rw, *, n_tok_all, row_off, carry=None, tm=256):
    B, T, _ = x.shape
    tm = min(tm, T)
    nt = T // tm
    tok = lambda width: pl.BlockSpec((None, tm, width), lambda b, i: (b, i, 0))
    act = lambda width, cb=0: pl.BlockSpec((None, tm, width), lambda b, i: (b, i, cb))
    if kind == "even":
        of, ob, P, hf, hb = acts
        a_in = [of, ob, P, hf, hb, P]
        a_specs = [act(GLA_V), act(GLA_V), act(GLA_V, 2), act(LRU_W), act(LRU_W), act(LRU_W, 4)]
        body = _out_even_kernel
    else:
        halves, P = acts
        a_in = list(halves) + [P]
        a_specs = [act(ML_H * LANES)] * 4 + [act(ML_V, 2)]
        body = _out_odd_kernel
    w_in = [mvec, gg.reshape(1, -1), wout, lng.reshape(1, D), lnb.reshape(1, D)] + list(rw)
    w_specs = [pl.BlockSpec((None, SUBLANES, D), lambda b, i: (b, 0, 0))] + [_const_spec(a.shape) for a in w_in[1:]]
    rb = row_off // tm
    tab = lambda rows: pl.BlockSpec((tm * rows, LANES), lambda b, i: (rb + b * nt + i, 0))
    tab_shape = lambda rows, dt: jax.ShapeDtypeStruct((n_tok_all * rows, LANES), dt)
    out_specs = [tok(D), tab(TOK_ROWS), tab(1), tab(1)]
    out_shape = [jax.ShapeDtypeStruct(x.shape, F32), tab_shape(TOK_ROWS, F32), tab_shape(1, jnp.int32), tab_shape(1, F32)]
    c_in, c_specs, aliases = [], [], {}
    if carry is None and n_tok_all != B * T:
        carry = tuple(jnp.zeros(s.shape, s.dtype) for s in out_shape[1:])
    if carry is not None:
        c_in = list(carry)
        c_specs = [pl.BlockSpec(memory_space=pl.ANY)] * 3
        base = len(a_in) + 1 + len(w_in)
        aliases = {base: 1, base + 1: 2, base + 2: 3}
    x1, xm, eid, gate = pl.pallas_call(
        body,
        grid=(B, nt),
        in_specs=a_specs + [tok(D)] + w_specs + c_specs,
        out_specs=out_specs,
        out_shape=out_shape,
        input_output_aliases=aliases,
        compiler_params=_params(2),
        name="out_" + kind,
    )(*a_in, x, *w_in, *c_in)
    return x1, xm, eid, gate


def _expert_kernel(be_ref, src_c, src_n, dst_p, dst_c, g_ref, wg, wu, wd, xm_hbm, y_hbm, xbuf, ybuf, gs, ss,
                   *, nb, n_slots):
    del be_ref
    n = pl.program_id(0)
    blk, tr, tp = MOE_BLK, TOK_ROWS, TOK_PITCH

    def gather(idx_ref, slot):
        for r in range(blk):
            row = pl.multiple_of(idx_ref[0, r], tr)
            pltpu.make_async_copy(xm_hbm.at[pl.ds(row, tr)], xbuf.at[slot, pl.ds(r * tp, tr)], gs.at[slot]).start()

    def scatter(idx_ref, slot):
        for r in range(blk):
            row = pl.multiple_of(idx_ref[0, r], tr)
            pltpu.make_async_copy(ybuf.at[slot, pl.ds(r * tp, tr)], y_hbm.at[pl.ds(row, tr)], ss.at[slot]).start()

    def wait_gather(slot):
        pltpu.make_async_copy(xm_hbm.at[pl.ds(0, blk * tr)], xbuf.at[slot, pl.ds(0, blk * tr)], gs.at[slot]).wait()

    def wait_scatter(slot):
        pltpu.make_async_copy(ybuf.at[slot, pl.ds(0, blk * tr)], y_hbm.at[pl.ds(0, blk * tr)], ss.at[slot]).wait()

    def mlp(slot):
        xs, ys = xbuf.at[slot], ybuf.at[slot]
        xb = jnp.concatenate([xs[pl.ds(j, blk, stride=tp), :] for j in range(tr)], axis=1).astype(BF16)
        hg = jnp.dot(xb, wg[...].astype(BF16), preferred_element_type=F32)
        hu = jnp.dot(xb, wu[...].astype(BF16), preferred_element_type=F32)
        h = (_silu(hg) * hu).astype(BF16)
        y = jnp.dot(h, wd[...].astype(BF16), preferred_element_type=F32) * g_ref[...]
        for j in range(tr):
            ys[pl.ds(j, blk, stride=tp), :] = y[:, j * LANES:(j + 1) * LANES]

    s = lax.rem(n, 2)

    @pl.when(n == 0)
    def _():
        ybuf[...] = jnp.zeros_like(ybuf)
        pltpu.make_async_copy(ybuf.at[0, pl.ds(0, blk * tr)], y_hbm.at[pl.ds((n_slots + blk) * tr, blk * tr)],
                              ss.at[0]).start()
        gather(src_c, 0)

    wait_gather(s)
    wait_scatter(s)
    gather(src_n, 1 - s)
    scatter(dst_p, 1 - s)
    mlp(s)

    @pl.when(n == nb - 1)
    def _():
        scatter(dst_c, s)
        wait_gather(1 - s)
        wait_scatter(1 - s)
        wait_scatter(s)


def moe_experts(xm, block_e, slot_src, slot_dst, slot_gate, wg, wu, wd, layer):
    nb, blk = slot_src.shape
    n_slots = nb * blk
    src3 = (slot_src * TOK_ROWS).reshape(nb, 1, blk)
    spare = (n_slots + jnp.arange(blk, dtype=jnp.int32)).reshape(1, 1, blk)
    dst3 = jnp.concatenate([spare, slot_dst.reshape(nb, 1, blk)], axis=0) * TOK_ROWS
    gate3 = slot_gate.reshape(nb, blk, 1)
    smem_blk = lambda f: pl.BlockSpec((None, 1, blk), f, memory_space=pltpu.SMEM)
    grid_spec = pltpu.PrefetchScalarGridSpec(
        num_scalar_prefetch=1,
        grid=(nb,),
        in_specs=[
            smem_blk(lambda n, be: (n, 0, 0)),
            smem_blk(lambda n, be: (jnp.minimum(n + 1, nb - 1), 0, 0)),
            smem_blk(lambda n, be: (n, 0, 0)),
            smem_blk(lambda n, be: (n + 1, 0, 0)),
            pl.BlockSpec((None, blk, 1), lambda n, be: (n, 0, 0)),
            pl.BlockSpec((None, None, D, D_EXP), lambda n, be: (layer, be[n], 0, 0)),
            pl.BlockSpec((None, None, D, D_EXP), lambda n, be: (layer, be[n], 0, 0)),
            pl.BlockSpec((None, None, D_EXP, D), lambda n, be: (layer, be[n], 0, 0)),
            pl.BlockSpec(memory_space=pl.ANY),
        ],
        out_specs=pl.BlockSpec(memory_space=pl.ANY),
        scratch_shapes=[
            pltpu.VMEM((2, blk * TOK_PITCH, LANES), F32),
            pltpu.VMEM((2, blk * TOK_PITCH, LANES), F32),
            pltpu.SemaphoreType.DMA((2,)),
            pltpu.SemaphoreType.DMA((2,)),
        ],
    )
    return pl.pallas_call(
        functools.partial(_expert_kernel, nb=nb, n_slots=n_slots),
        grid_spec=grid_spec,
        out_shape=jax.ShapeDtypeStruct(((n_slots + 2 * blk) * TOK_ROWS, LANES), F32),
        compiler_params=_params(1),
        name="moe_experts",
    )(block_e, src3, src3, dst3, dst3, gate3, wg, wu, wd, xm)


def moe_tables(eid, gate, n_tok):
    blk = MOE_BLK
    n_assign = 2 * n_tok
    flat_e = eid[:, :2].reshape(-1)
    flat_g = gate[:, :2].reshape(-1)
    counts = jnp.sum((flat_e[:, None] == jnp.arange(N_EXP, dtype=jnp.int32)[None, :]).astype(jnp.int32), axis=0)
    padded = (counts + blk - 1) // blk * blk
    pad_end = jnp.cumsum(padded)
    pad_start = pad_end - padded
    cnt_start = jnp.cumsum(counts) - counts
    order = jnp.argsort(flat_e).astype(jnp.int32)
    nb = -(-(n_assign + N_EXP * (blk - 1)) // blk)
    bstart = jnp.arange(nb, dtype=jnp.int32) * blk
    block_e = jnp.minimum(jnp.sum((pad_end[None, :] <= bstart[:, None]).astype(jnp.int32), axis=1), N_EXP - 1)
    lane = jnp.arange(blk, dtype=jnp.int32)[None, :]
    j = (bstart - pad_start[block_e])[:, None] + lane
    cnt_b = counts[block_e][:, None]
    valid = j < cnt_b
    n_valid_before = cnt_start[block_e][:, None] + jnp.minimum(j, cnt_b)
    a = order[jnp.clip(n_valid_before, 0, n_assign - 1)]
    slot = bstart[:, None] + lane
    slot_src = jnp.where(valid, a // 2, 0).astype(jnp.int32)
    slot_dst = jnp.where(valid, (a % 2) * n_tok + a // 2, n_assign + slot - n_valid_before).astype(jnp.int32)
    slot_gate = jnp.where(valid, flat_g[a], 0.0)
    return block_e.astype(jnp.int32), slot_src, slot_dst, slot_gate


def _combine_kernel(x_ref, y0_ref, y1_ref, m_ref, g_ref, b_ref, o_ref):
    tm = x_ref.shape[0]
    y = jnp.concatenate([y0_ref[pl.ds(j, tm, stride=TOK_ROWS), :] + y1_ref[pl.ds(j, tm, stride=TOK_ROWS), :]
                         for j in range(TOK_ROWS)], axis=1)
    o_ref[...] = _layer_norm(ALPHA * x_ref[...] + m_ref[5:6, :] * y, g_ref[...], b_ref[...])


def moe_combine(x1, Y, mvec, lng, lnb, *, n_tok, row_off, tm=512):
    B, T, _ = x1.shape
    tm = min(tm, T)
    nt = T // tm
    rb = row_off // tm
    kb = n_tok // tm
    return pl.pallas_call(
        _combine_kernel,
        grid=(B, nt),
        in_specs=[
            pl.BlockSpec((None, tm, D), lambda b, i: (b, i, 0)),
            pl.BlockSpec((tm * TOK_ROWS, LANES), lambda b, i: (rb + b * nt + i, 0)),
            pl.BlockSpec((tm * TOK_ROWS, LANES), lambda b, i: (kb + rb + b * nt + i, 0)),
            pl.BlockSpec((None, SUBLANES, D), lambda b, i: (b, 0, 0)),
            _const_spec((1, D)), _const_spec((1, D)),
        ],
        out_specs=pl.BlockSpec((None, tm, D), lambda b, i: (b, i, 0)),
        out_shape=jax.ShapeDtypeStruct((B, T, D), F32),
        compiler_params=_params(2),
        name="moe_combine",
    )(x1, Y, Y, mvec, lng.reshape(1, D), lnb.reshape(1, D))


def _even_w_in(w):
    g0 = GLA_QK * 2 + GLA_V * 2
    g1 = g0 + 2 * GLA_RANK
    pad = jnp.zeros((D, EV_NP - w.shape[1]), w.dtype)
    return jnp.concatenate([w[:, :g0], w[:, g1:], w[:, g0:g1], pad], axis=1).astype(BF16)


def _odd_w_in(w):
    return jnp.concatenate([w, jnp.zeros((D, OD_NP - w.shape[1]), w.dtype)], axis=1).astype(BF16)


def _lru_weights(w_r, b_r, w_i, b_i):
    wdir = jnp.concatenate([w_r, w_i], axis=-1).astype(BF16)
    blk = lambda b: b.reshape(2, LRU_NB, 1, LRU_BW)
    bdir = jnp.concatenate([blk(b_r), blk(b_i)], axis=-1)
    return wdir, bdir


def _gla_gate_weights(w_a2, b_a):
    wa = jnp.zeros((2, LANES, GLA_QK), F32)
    wa = wa.at[0, :GLA_RANK].set(w_a2[0]).at[1, GLA_RANK:2 * GLA_RANK].set(w_a2[1])
    w_hi, w_lo = _split2(wa)
    return jnp.concatenate([w_hi, w_lo, w_hi], axis=1), b_a.reshape(2, 1, GLA_QK)


def _mlstm_gate_table(P, n_col):
    B, T, _ = P.shape
    g = P[:, :, OD_GATE_COL:OD_GATE_COL + 4 * ML_H].reshape(B, T // n_col, n_col, 4, ML_H)
    g = jnp.transpose(g, (0, 4, 2, 3, 1))
    return jnp.concatenate([g, jnp.zeros_like(g)], axis=3)


def _mvec(mods_l, rows):
    m6 = jnp.stack([mods_l[r] for r in rows]).reshape(len(rows), 6, D)
    return jnp.concatenate([m6, jnp.zeros((len(rows), 2, D), F32)], axis=1)


def kernel(x, c, ctx, c_ctx, w_mod, b_mod, ln_g, ln_b, ev_w_in, gla_w_a2, gla_b_a, gla_norm, lru_conv_w, lru_conv_b, lru_w_r, lru_b_r, lru_w_i, lru_b_i, lru_lam, ev_w_out, od_w_in, mlstm_b_gate, mlstm_norm, od_w_out, moe_w_group, moe_b_group, moe_w_expert, moe_b_expert, moe_w_gate, moe_w_up, moe_w_down):
    B, S, _ = x.shape
    T_ctx = ctx.shape[1]
    cond8 = jnp.zeros((SUBLANES, D), F32).at[:B].set(c).at[B].set(c_ctx)
    mods = cond_mod(cond8, w_mod, b_mod)
    h_ctx = ctx
    for l in range(DEPTH):
        j = l // 2
        last = l == DEPTH - 1
        m_lat = _mvec(mods[l], list(range(B)))
        m_ctx = _mvec(mods[l], [B] * B)
        rw = _router_weights(moe_w_group[l], moe_b_group[l], moe_w_expert[l], moe_b_expert[l])
        n_tok = B * S if last else B * (S + T_ctx)
        if l % 2 == 0:
            w_in = _even_w_in(ev_w_in[j])
            wa, ba = _gla_gate_weights(gla_w_a2[j], gla_b_a[j])
            wdir, bdir = _lru_weights(lru_w_r[j], lru_b_r[j], lru_w_i[j], lru_b_i[j])
            w_out = ev_w_out[j].astype(BF16)

            def mix_even(xin, mv, s0, h0):
                P = inproj(xin, mv, w_in, tn=EV_TN)
                of, ob, sfin = gla_scan(P, wa, ba, s0)
                hf, hb, hfin = lru_mix(P, lru_conv_w[j], lru_conv_b[j], wdir, bdir, lru_lam[j], h0)
                return (of, ob, P, hf, hb), sfin, hfin

            s0 = jnp.zeros((B, 2, GLA_H, GLA_DV, GLA_DK), F32)
            h0 = jnp.zeros((B, 2, SUBLANES, LRU_W), F32)
            acts_c, s_c, h_c = mix_even(h_ctx, m_ctx, s0, h0)
            acts_l, _, _ = mix_even(x, m_lat, s_c, h_c)
            carry = None
            if not last:
                hc1, *carry = out_block("even", acts_c, h_ctx, m_ctx, gla_norm[j], w_out, ln_g[l, 0], ln_b[l, 0], rw,
                                        n_tok_all=n_tok, row_off=B * S)
            x1, xm, eid, gate = out_block("even", acts_l, x, m_lat, gla_norm[j], w_out, ln_g[l, 0], ln_b[l, 0], rw,
                                          n_tok_all=n_tok, row_off=0, carry=carry)
        else:
            w_in = _odd_w_in(od_w_in[j])
            w_out = od_w_out[j].astype(BF16)
            bgate = jnp.concatenate([mlstm_b_gate[j].T, jnp.zeros((ML_H, 4), F32)], axis=1).reshape(ML_H, SUBLANES, 1)

            def mix_odd(xin, mv, c0, m0, n_col):
                P = inproj(xin, mv, w_in, tn=OD_TN)
                gt = _mlstm_gate_table(P, n_col)
                halves, cfin, mfin = mlstm_scan(P, gt, bgate, c0, m0, n_col=n_col)
                return (halves, P), cfin, mfin

            c0 = jnp.zeros((B, 2, ML_H, ML_DK, ML_AUG), F32)
            m0 = jnp.full((B, 2, ML_H, SUBLANES, LANES), M_INIT, F32)
            acts_c, c_c, m_c = mix_odd(h_ctx, m_ctx, c0, m0, 1)
            acts_l, _, _ = mix_odd(x, m_lat, c_c, m_c, GRID_W)
            carry = None
            if not last:
                hc1, *carry = out_block("odd", acts_c, h_ctx, m_ctx, mlstm_norm[j], w_out, ln_g[l, 0], ln_b[l, 0], rw,
                                        n_tok_all=n_tok, row_off=B * S)
            x1, xm, eid, gate = out_block("odd", acts_l, x, m_lat, mlstm_norm[j], w_out, ln_g[l, 0], ln_b[l, 0], rw,
                                          n_tok_all=n_tok, row_off=0, carry=carry)
        block_e, slot_src, slot_dst, slot_gate = moe_tables(eid, gate, n_tok)
        Y = moe_experts(xm, block_e, slot_src, slot_dst, slot_gate, moe_w_gate, moe_w_up, moe_w_down, l)
        x = moe_combine(x1, Y, m_lat, ln_g[l, 1], ln_b[l, 1], n_tok=n_tok, row_off=0)
        if not last:
            h_ctx = moe_combine(hc1, Y, m_ctx, ln_g[l, 1], ln_b[l, 1], n_tok=n_tok, row_off=B * S)
    return x
```

```python
import functools

import jax
import jax.numpy as jnp
from jax import lax
from jax.experimental import pallas as pl
from jax.experimental.pallas import tpu as pltpu

F32 = jnp.float32
BF16 = jnp.bfloat16
HIGHEST = lax.Precision.HIGHEST

D = 2048
DEPTH = 2
GRID_W = 64
CHUNK = 64
GLA_H, GLA_DK, GLA_DV, GLA_RANK, GLA_TAU = 4, 128, 256, 16, 16.0
GLA_QK, GLA_V = GLA_H * GLA_DK, GLA_H * GLA_DV
LRU_W, LRU_NB, LRU_C, CONV_W = 1024, 8, 8.0, 4
LRU_BW = LRU_W // LRU_NB
ML_H, ML_DK, ML_DV = 8, 128, 256
ML_QK, ML_V = ML_H * ML_DK, ML_H * ML_DV
M_INIT = -1e30
N_GROUPS, EPG, N_EXP, D_EXP = 4, 8, 32, 512
ALPHA = (2 * DEPTH) ** 0.25
EPS = 1e-5

LANES = 128
SUBLANES = 8
VMEM_LIMIT = 56 * 1024 * 1024

EV_NP = 5376
EV_TN = 1792
EV_GATE_COL = 5120
OD_NP = 6400
OD_TN = 1280
OD_GATE_COL = 6144
MOE_BLK = 256
TOK_ROWS = D // LANES
TOK_PITCH = 24


def _params(n_axes):
    return pltpu.CompilerParams(dimension_semantics=("arbitrary",) * n_axes, vmem_limit_bytes=VMEM_LIMIT)


def _sigmoid(x):
    return 1.0 / (1.0 + jnp.exp(-x))


def _sigmoid_tanh(x):
    return 0.5 * jnp.tanh(0.5 * x) + 0.5


def _silu(x):
    return x * _sigmoid(x)


def _log_sigmoid(x):
    return jnp.minimum(x, 0.0) - jnp.log1p(jnp.exp(-jnp.abs(x)))


def _gelu_tanh(x):
    return 0.5 * x * (1.0 + jnp.tanh(0.7978845608028654 * (x + 0.044715 * (x * x * x))))


def _bdot(a, b):
    return jnp.dot(a.astype(BF16), b.astype(BF16), preferred_element_type=F32)


def _bdot_nt(a, b):
    return lax.dot_general(a.astype(BF16), b.astype(BF16), (((1,), (1,)), ((), ())), preferred_element_type=F32)


def _bdot_tn(a, b):
    return lax.dot_general(a.astype(BF16), b.astype(BF16), (((0,), (0,)), ((), ())), preferred_element_type=F32)


def _split2(x):
    hi = x.astype(BF16)
    return hi, (x - hi.astype(F32)).astype(BF16)


def _split3(x):
    hi = x.astype(BF16)
    r = x - hi.astype(F32)
    mid = r.astype(BF16)
    return hi, mid, (r - mid.astype(F32)).astype(BF16)


def _mod_kernel(c_ref, w_ref, b_ref, o_ref):
    o_ref[...] = _bdot(_silu(c_ref[...]), w_ref[...]) + b_ref[...]


def cond_mod(cond8, w_mod, b_mod):
    nl, _, n6 = w_mod.shape
    tn = 1024
    return pl.pallas_call(
        _mod_kernel,
        grid=(nl, n6 // tn),
        in_specs=[
            pl.BlockSpec((SUBLANES, D), lambda l, j: (0, 0)),
            pl.BlockSpec((None, D, tn), lambda l, j: (l, 0, j)),
            pl.BlockSpec((None, 1, tn), lambda l, j: (l, 0, j)),
        ],
        out_specs=pl.BlockSpec((None, SUBLANES, tn), lambda l, j: (l, 0, j)),
        out_shape=jax.ShapeDtypeStruct((nl, SUBLANES, n6), F32),
        compiler_params=_params(2),
        name="cond_mod",
    )(cond8, w_mod, b_mod.reshape(nl, 1, n6))


def _inproj_kernel(x_ref, m_ref, w_ref, o_ref, u_ref):
    @pl.when(pl.program_id(2) == 0)
    def _():
        u_ref[...] = (x_ref[...] * (1.0 + m_ref[1:2, :]) + m_ref[0:1, :]).astype(BF16)

    o_ref[...] = jnp.dot(u_ref[...], w_ref[...], preferred_element_type=F32)


def inproj(x3, mvec, w, *, tn, tm=1024):
    B, T, _ = x3.shape
    Np = w.shape[1]
    tm = min(tm, T)
    return pl.pallas_call(
        _inproj_kernel,
        grid=(B, T // tm, Np // tn),
        in_specs=[
            pl.BlockSpec((None, tm, D), lambda b, i, j: (b, i, 0)),
            pl.BlockSpec((None, SUBLANES, D), lambda b, i, j: (b, 0, 0)),
            pl.BlockSpec((D, tn), lambda b, i, j: (0, j)),
        ],
        out_specs=pl.BlockSpec((None, tm, tn), lambda b, i, j: (b, i, j)),
        out_shape=jax.ShapeDtypeStruct((B, T, Np), F32),
        scratch_shapes=[pltpu.VMEM((tm, D), BF16)],
        compiler_params=_params(3),
        name="inproj",
    )(x3, mvec, w)


def _gla_kernel(qf, kf, vf, gf, qb, kb, vb, gb, wa_ref, ba_ref, s0_ref, of_ref, ob_ref, sfin_ref, st_ref, *, tb, ns):
    s = pl.program_id(1)

    @pl.when(s == 0)
    def _():
        st_ref[...] = s0_ref[...]

    L = CHUNK
    row = lax.broadcasted_iota(jnp.int32, (L, L), 0)
    col = lax.broadcasted_iota(jnp.int32, (L, L), 1)
    row3 = lax.broadcasted_iota(jnp.int32, (L, 3 * L), 0)
    col3 = lax.rem(lax.broadcasted_iota(jnp.int32, (L, 3 * L), 1), L)
    scale = GLA_DK ** -0.5
    for d, (q_ref, k_ref, v_ref, g_ref, o_ref) in enumerate(((qf, kf, vf, gf, of_ref), (qb, kb, vb, gb, ob_ref))):
        mask = (row >= col) if d == 0 else (row <= col)
        m3 = jnp.where((row3 >= col3) if d == 0 else (row3 <= col3), 1.0, 0.0).astype(BF16)
        x_hi, x_lo = _split2(g_ref[...])
        gpre = jnp.dot(jnp.concatenate([x_hi, x_hi, x_lo], axis=1), wa_ref[d], preferred_element_type=F32)
        glog = _log_sigmoid(gpre + ba_ref[d]) * (1.0 / GLA_TAU)
        nchunk = tb // L
        for c in (range(nchunk) if d == 0 else range(nchunk - 1, -1, -1)):
            r0 = c * L
            g1, g2, g3 = _split3(glog[r0:r0 + L, :])
            b_all = jnp.dot(m3, jnp.concatenate([g1, g2, g3], axis=0), preferred_element_type=F32)
            for h in range(GLA_H):
                q = q_ref[r0:r0 + L, h * GLA_DK:(h + 1) * GLA_DK]
                k = k_ref[r0:r0 + L, h * GLA_DK:(h + 1) * GLA_DK]
                v = v_ref[r0:r0 + L, h * GLA_DV:(h + 1) * GLA_DV].astype(BF16)
                b = b_all[:, h * GLA_DK:(h + 1) * GLA_DK]
                bl = b[L - 1:L, :] if d == 0 else b[0:1, :]
                qd = (q * (jnp.exp(b) * scale)).astype(BF16)
                ki = k * jnp.exp(-b)
                ke = k * jnp.exp(bl - b)
                att = jnp.where(mask, _bdot_nt(qd, ki), 0.0)
                st = st_ref[d, h]
                o = _bdot(att, v) + _bdot_nt(qd, st)
                o_ref[r0:r0 + L, h * GLA_DV:(h + 1) * GLA_DV] = o
                st_ref[d, h] = st * jnp.exp(bl) + _bdot_tn(v, ke)

    @pl.when(s == ns - 1)
    def _():
        sfin_ref[...] = st_ref[...]


def gla_scan(P, wa, ba, s0, *, tb=256):
    B, T, _ = P.shape
    tb = min(tb, T)
    ns = T // tb
    fwd = lambda b, s: s
    bwd = lambda b, s: ns - 1 - s

    def specs(pos):
        return [
            pl.BlockSpec((None, tb, GLA_QK), lambda b, s: (b, pos(b, s), 0)),
            pl.BlockSpec((None, tb, GLA_QK), lambda b, s: (b, pos(b, s), 1)),
            pl.BlockSpec((None, tb, GLA_V), lambda b, s: (b, pos(b, s), 1)),
            pl.BlockSpec((None, tb, LANES), lambda b, s: (b, pos(b, s), EV_GATE_COL // LANES)),
        ]

    st_shape = (2, GLA_H, GLA_DV, GLA_DK)
    st_spec = pl.BlockSpec((None,) + st_shape, lambda b, s: (b, 0, 0, 0, 0))
    return pl.pallas_call(
        functools.partial(_gla_kernel, tb=tb, ns=ns),
        grid=(B, ns),
        in_specs=specs(fwd) + specs(bwd) + [
            pl.BlockSpec((2, 3 * LANES, GLA_QK), lambda b, s: (0, 0, 0)),
            pl.BlockSpec((2, 1, GLA_QK), lambda b, s: (0, 0, 0)),
            st_spec,
        ],
        out_specs=[
            pl.BlockSpec((None, tb, GLA_V), lambda b, s: (b, s, 0)),
            pl.BlockSpec((None, tb, GLA_V), lambda b, s: (b, ns - 1 - s, 0)),
            st_spec,
        ],
        out_shape=[
            jax.ShapeDtypeStruct((B, T, GLA_V), F32),
            jax.ShapeDtypeStruct((B, T, GLA_V), F32),
            jax.ShapeDtypeStruct((B,) + st_shape, F32),
        ],
        scratch_shapes=[pltpu.VMEM(st_shape, F32)],
        compiler_params=_params(2),
        name="gla_scan",
    )(P, P, P, P, P, P, P, P, wa, ba, s0)


def _lru_kernel(xf, pf, nf, xb, pb, nb_, cw_ref, cb_ref, w_ref, bias_ref, lam_ref, h0_ref,
                hf_ref, hb_ref, hfin_ref, a_s, u_s, h_s, *, tm, nt):
    i = pl.program_id(1)

    @pl.when(i == 0)
    def _():
        h_s[...] = h0_ref[...]

    cw = cw_ref[...]
    off = SUBLANES - CONV_W // 2
    for d, (x_ref, p_ref, n_ref) in enumerate(((xf, pf, nf), (xb, pb, nb_))):
        ti = i if d == 0 else nt - 1 - i
        prev = jnp.where(ti > 0, p_ref[...], 0.0)
        nxt = jnp.where(ti < nt - 1, n_ref[...], 0.0)
        ext = jnp.concatenate([prev, x_ref[...], nxt], axis=0)
        xc = cb_ref[...] + (ext[off:off + tm] * cw[0:1] + ext[off + 1:off + 1 + tm] * cw[1:2]
                            + ext[off + 2:off + 2 + tm] * cw[2:3] + ext[off + 3:off + 3 + tm] * cw[3:4])
        for n in range(LRU_NB):
            cs = slice(n * LRU_BW, (n + 1) * LRU_BW)
            xn = xc[:, cs]
            z = _bdot(xn, w_ref[d, n]) + bias_ref[d, n]
            r = _sigmoid_tanh(z[:, :LRU_BW])
            ig = _sigmoid_tanh(z[:, LRU_BW:])
            a = jnp.exp((LRU_C * r) * _log_sigmoid(lam_ref[d:d + 1, cs]))
            a_s[d, :, cs] = a
            u_s[d, :, cs] = jnp.sqrt(1.0 - a * a) * (ig * xn)

    sub = lax.broadcasted_iota(jnp.int32, (SUBLANES, LRU_W), 0)

    def group(g, carry):
        hf, hb = carry
        bf = pl.multiple_of(g * SUBLANES, SUBLANES)
        bb = pl.multiple_of(tm - SUBLANES - g * SUBLANES, SUBLANES)
        a_f, u_f = a_s[0, pl.ds(bf, SUBLANES), :], u_s[0, pl.ds(bf, SUBLANES), :]
        a_b, u_b = a_s[1, pl.ds(bb, SUBLANES), :], u_s[1, pl.ds(bb, SUBLANES), :]
        for j in range(SUBLANES):
            hf = jnp.where(sub == j, a_f * pltpu.roll(hf, 1, axis=0) + u_f, hf)
            jb = SUBLANES - 1 - j
            hb = jnp.where(sub == jb, a_b * pltpu.roll(hb, SUBLANES - 1, axis=0) + u_b, hb)
        hf_ref[pl.ds(bf, SUBLANES), :] = hf
        hb_ref[pl.ds(bb, SUBLANES), :] = hb
        return hf, hb

    hf, hb = lax.fori_loop(0, tm // SUBLANES, group, (h_s[0], h_s[1]))
    h_s[0] = hf
    h_s[1] = hb

    @pl.when(i == nt - 1)
    def _():
        hfin_ref[...] = h_s[...]


def lru_mix(P, conv_w, conv_b, wdir, bdir, lam, h0, *, tm=512):
    B, T, _ = P.shape
    tm = min(tm, T)
    nt = T // tm
    xcol = 3
    r8 = tm // SUBLANES
    n8 = T // SUBLANES
    fwd = lambda i: i
    bwd = lambda i: nt - 1 - i

    def xspecs(pos):
        return [
            pl.BlockSpec((None, tm, LRU_W), lambda b, i: (b, pos(i), xcol)),
            pl.BlockSpec((None, SUBLANES, LRU_W), lambda b, i: (b, jnp.maximum(pos(i) * r8 - 1, 0), xcol)),
            pl.BlockSpec((None, SUBLANES, LRU_W), lambda b, i: (b, jnp.minimum((pos(i) + 1) * r8, n8 - 1), xcol)),
        ]

    hspec = pl.BlockSpec((None, 2, SUBLANES, LRU_W), lambda b, i: (b, 0, 0, 0))
    seq = jax.ShapeDtypeStruct((B, T, LRU_W), F32)
    return pl.pallas_call(
        functools.partial(_lru_kernel, tm=tm, nt=nt),
        grid=(B, nt),
        in_specs=xspecs(fwd) + xspecs(bwd) + [
            _const_spec((CONV_W, LRU_W)), _const_spec((1, LRU_W)),
            _const_spec((2, LRU_NB, LRU_BW, 2 * LRU_BW)), _const_spec((2, LRU_NB, 1, 2 * LRU_BW)),
            _const_spec((2, LRU_W)), hspec,
        ],
        out_specs=[
            pl.BlockSpec((None, tm, LRU_W), lambda b, i: (b, i, 0)),
            pl.BlockSpec((None, tm, LRU_W), lambda b, i: (b, nt - 1 - i, 0)),
            hspec,
        ],
        out_shape=[seq, seq, jax.ShapeDtypeStruct((B, 2, SUBLANES, LRU_W), F32)],
        scratch_shapes=[pltpu.VMEM((2, tm, LRU_W), F32), pltpu.VMEM((2, tm, LRU_W), F32),
                        pltpu.VMEM((2, SUBLANES, LRU_W), F32)],
        compiler_params=_params(2),
        name="lru_mix",
    )(P, P, P, P, P, P, conv_w, conv_b.reshape(1, LRU_W), wdir, bdir, lam, h0)


ML_AUG = ML_DV + LANES
ML_L = 256


def _chunk_rows(ref, w):
    L, nwl, width = ref.shape
    flat = ref.reshape(L * nwl, width)
    return flat.at[pl.ds(w, L, stride=nwl), :] if nwl > 1 else flat


def _mlstm_kernel(qf, kf, vf0, vf1, gf, qb, kb, vb0, vb1, gb, bg_ref, c0_ref, m0_ref,
                  hf0_ref, hf1_ref, hb0_ref, hb1_ref, cfin_ref, mfin_ref, c_ref, m_ref, *, L, nwl, ns):
    s = pl.program_id(2)

    @pl.when(s == 0)
    def _():
        c_ref[...] = c0_ref[...]
        m_ref[...] = m0_ref[...]

    row = lax.broadcasted_iota(jnp.int32, (L, L), 0)
    col = lax.broadcasted_iota(jnp.int32, (L, L), 1)
    nr = nwl * SUBLANES
    sub = lax.rem(lax.broadcasted_iota(jnp.int32, (nr, L), 0), SUBLANES)
    lane = lax.broadcasted_iota(jnp.int32, (nr, L), 1)
    is_f = (sub == 1) | (sub == 3)
    ones_col = (lax.broadcasted_iota(jnp.int32, (L, LANES), 1) == 0).astype(BF16)
    kscale = ML_DK ** -0.5
    bias = jnp.concatenate([bg_ref[...]] * nwl, axis=0)

    chunk_rows = _chunk_rows

    def gate_vectors(g_ref, d):
        pre = g_ref[...].reshape(nr, L) + bias
        xg = jnp.where(is_f, _log_sigmoid(pre), pre)
        cs = xg
        k_ = 1
        while k_ < L:
            if d == 0:
                cs = cs + jnp.where(lane >= k_, pltpu.roll(cs, k_, axis=1), 0.0)
            else:
                cs = cs + jnp.where(lane < L - k_, pltpu.roll(cs, L - k_, axis=1), 0.0)
            k_ *= 2
        ri, rf = 2 * d, 2 * d + 1
        pad = jnp.zeros((LANES - nr, L), F32)
        cs_t = jnp.concatenate([cs, pad], axis=0).T
        xg_t = jnp.concatenate([xg, pad], axis=0).T
        out = []
        for w in range(nwl):
            jb, ji = SUBLANES * w + rf, SUBLANES * w + ri
            out.append((cs[jb:jb + 1, :], xg[ji:ji + 1, :], cs_t[:, jb:jb + 1], xg_t[:, ji:ji + 1]))
        return out

    vec = (gate_vectors(gf, 0), gate_vectors(gb, 1))
    work = [(0, w, qf, kf, (vf0, vf1), (hf0_ref, hf1_ref)) for w in range(nwl)]
    work_b = [(1, w, qb, kb, (vb0, vb1), (hb0_ref, hb1_ref)) for w in range(nwl - 1, -1, -1)]
    for pair in zip(work, work_b):
      for d, w, q_ref, k_ref, v_refs, h_refs in pair:
        mask = (row >= col) if d == 0 else (row <= col)
        b_row, i_row, b_col, i_col = vec[d][w]
        b_last = b_row[:, L - 1:L] if d == 0 else b_row[:, 0:1]
        dmat = jnp.where(mask, b_col + (i_row - b_row), -jnp.inf)
        m_loc = jnp.max(dmat, axis=1, keepdims=True)
        q = chunk_rows(q_ref, w)[...].astype(BF16)
        k = chunk_rows(k_ref, w)[...] * kscale
        vaug = jnp.concatenate([chunk_rows(v_refs[0], w)[...].astype(BF16), chunk_rows(v_refs[1], w)[...].astype(BF16),
                                ones_col], axis=1)
        sc = _bdot_nt(q, k) * jnp.exp(dmat - m_loc)
        n_loc = _bdot(sc, vaug)
        g_row = b_last - b_row + i_row
        m_k = jnp.max(g_row, axis=1, keepdims=True)
        kv_loc = _bdot_tn(jnp.exp(b_last - b_col + i_col - m_k) * k, vaug)
        m = m_ref[d][0:1, 0:1]
        caug = c_ref[d]
        inter = b_col + m
        m_t = jnp.maximum(inter, m_loc)
        acc = jnp.exp(m_loc - m_t) * n_loc + jnp.exp(inter - m_t) * _bdot(q, caug)
        den = acc[:, ML_DV:ML_DV + 1]
        h = acc[:, :ML_DV] / jnp.maximum(jnp.abs(den), jnp.exp(-m_t))
        chunk_rows(h_refs[0], w)[...] = h[:, :LANES]
        chunk_rows(h_refs[1], w)[...] = h[:, LANES:]
        m_new = jnp.maximum(b_last + m, m_k)
        c_ref[d] = jnp.exp(b_last + m - m_new) * caug + jnp.exp(m_k - m_new) * kv_loc
        m_ref[d] = jnp.broadcast_to(m_new, (SUBLANES, LANES))

    @pl.when(s == ns - 1)
    def _():
        cfin_ref[...] = c_ref[...]
        mfin_ref[...] = m_ref[...]


def mlstm_scan(P, gt, bgate, c0, m0, *, n_col):
    B, T, Np = P.shape
    L = T // n_col
    nwl = min(n_col, SUBLANES)
    ns = n_col // nwl
    P5 = P.reshape(B, L, ns, nwl, Np)
    fwd = lambda s: s
    bwd = lambda s: ns - 1 - s

    def specs(pos):
        return [
            pl.BlockSpec((None, L, None, nwl, ML_DK), lambda b, h, s: (b, 0, pos(s), 0, h)),
            pl.BlockSpec((None, L, None, nwl, ML_DK), lambda b, h, s: (b, 0, pos(s), 0, ML_H + h)),
            pl.BlockSpec((None, L, None, nwl, LANES), lambda b, h, s: (b, 0, pos(s), 0, 2 * ML_H + 2 * h)),
            pl.BlockSpec((None, L, None, nwl, LANES), lambda b, h, s: (b, 0, pos(s), 0, 2 * ML_H + 2 * h + 1)),
            pl.BlockSpec((None, None, nwl, SUBLANES, L), lambda b, h, s: (b, h, pos(s), 0, 0)),
        ]

    cshape = (2, ML_DK, ML_AUG)
    mshape = (2, SUBLANES, LANES)
    cspec = pl.BlockSpec((None, 2, None, ML_DK, ML_AUG), lambda b, h, s: (b, 0, h, 0, 0))
    mspec = pl.BlockSpec((None, 2, None, SUBLANES, LANES), lambda b, h, s: (b, 0, h, 0, 0))
    seq = jax.ShapeDtypeStruct((B, L, ns, nwl, ML_H * LANES), F32)
    ospec = lambda pos: pl.BlockSpec((None, L, None, nwl, LANES), lambda b, h, s: (b, 0, pos(s), 0, h))
    hf0, hf1, hb0, hb1, cfin, mfin = pl.pallas_call(
        functools.partial(_mlstm_kernel, L=L, nwl=nwl, ns=ns),
        grid=(B, ML_H, ns),
        in_specs=specs(fwd) + specs(bwd) + [
            pl.BlockSpec((None, SUBLANES, 1), lambda b, h, s: (h, 0, 0)),
            cspec, mspec,
        ],
        out_specs=[ospec(fwd), ospec(fwd), ospec(bwd), ospec(bwd), cspec, mspec],
        out_shape=[
            seq, seq, seq, seq,
            jax.ShapeDtypeStruct((B, 2, ML_H, ML_DK, ML_AUG), F32),
            jax.ShapeDtypeStruct((B, 2, ML_H, SUBLANES, LANES), F32),
        ],
        scratch_shapes=[pltpu.VMEM(cshape, F32), pltpu.VMEM(mshape, F32)],
        compiler_params=_params(3),
        name="mlstm_scan",
    )(P5, P5, P5, P5, gt, P5, P5, P5, P5, gt, bgate, c0, m0)
    flat = lambda a: a.reshape(B, T, ML_H * LANES)
    return (flat(hf0), flat(hf1), flat(hb0), flat(hb1)), cfin, mfin


def _layer_norm(z, g, b):
    mu = jnp.mean(z, axis=-1, keepdims=True)
    zc = z - mu
    var = jnp.mean(zc * zc, axis=-1, keepdims=True)
    return zc * lax.rsqrt(var + EPS) * g + b


def _head_rms(o, g, n_heads, width):
    parts = []
    for h in range(n_heads):
        oh = o[:, h * width:(h + 1) * width]
        parts.append(oh * lax.rsqrt(jnp.mean(oh * oh, axis=-1, keepdims=True) + EPS) * g)
    return jnp.concatenate(parts, axis=-1)


def _post(y, x_ref, m_ref, lng_ref, lnb_ref, rw_refs, x1_ref, xm_ref, eid_ref, gate_ref):
    w_hi, w_lo, bias = rw_refs
    x1 = _layer_norm(ALPHA * x_ref[...] + m_ref[2:3, :] * y, lng_ref[...], lnb_ref[...])
    x1_ref[...] = x1
    xm = x1 * (1.0 + m_ref[4:5, :]) + m_ref[3:4, :]
    tm = xm.shape[0]
    for j in range(TOK_ROWS):
        xm_ref[pl.ds(j, tm, stride=TOK_ROWS), :] = xm[:, j * LANES:(j + 1) * LANES]
    hi, lo = _split2(xm)
    lg = (jnp.dot(hi, w_hi[...], preferred_element_type=F32) + jnp.dot(hi, w_lo[...], preferred_element_type=F32)
          + jnp.dot(lo, w_hi[...], preferred_element_type=F32)) + bias[...]
    lane = lax.broadcasted_iota(jnp.int32, lg.shape, 1)
    neg = -jnp.inf
    glm = jnp.where(lane < N_GROUPS, lg, neg)
    gmax = jnp.max(glm, axis=-1, keepdims=True)
    gidx = jnp.min(jnp.where(glm == gmax, lane, LANES), axis=-1, keepdims=True)
    p_g = 1.0 / jnp.sum(jnp.exp(glm - gmax), axis=-1, keepdims=True)
    e_lane = lane - N_GROUPS
    in_group = (lax.shift_right_arithmetic(e_lane, 3) == gidx) & (e_lane < N_EXP)
    elm = jnp.where(in_group, lg, neg)
    m1 = jnp.max(elm, axis=-1, keepdims=True)
    i1 = jnp.min(jnp.where(elm == m1, e_lane, LANES), axis=-1, keepdims=True)
    elm2 = jnp.where(e_lane == i1, neg, elm)
    m2 = jnp.max(elm2, axis=-1, keepdims=True)
    i2 = jnp.min(jnp.where(elm2 == m2, e_lane, LANES), axis=-1, keepdims=True)
    e2 = jnp.exp(m2 - m1)
    g1 = p_g / (1.0 + e2)
    g2 = g1 * e2
    eid_ref[...] = jnp.where(lane == 0, i1, jnp.where(lane == 1, i2, 0)).T[:SUBLANES, :]
    gate_ref[...] = jnp.where(lane == 0, g1, jnp.where(lane == 1, g2, 0.0)).T[:SUBLANES, :]


def _out_even_kernel(of, ob, r, hf, hb, xg, x, m, gg, wout, lng, lnb, w_hi, w_lo, bias, *rest):
    x1_ref, xm_ref, eid_ref, gate_ref = rest[-4:]
    rg = r[...]
    o = _head_rms(of[...] + ob[...], gg[...], GLA_H, GLA_DV) * (rg * _sigmoid_tanh(rg))
    y2 = (hf[...] + hb[...]) * _gelu_tanh(xg[...])
    mix = jnp.concatenate([o.astype(BF16), y2.astype(BF16)], axis=-1)
    y = jnp.dot(mix, wout[...], preferred_element_type=F32)
    _post(y, x, m, lng, lnb, (w_hi, w_lo, bias), x1_ref, xm_ref, eid_ref, gate_ref)


def _out_odd_kernel(hf0, hf1, hb0, hb1, og, x, m, gg, wout, lng, lnb, w_hi, w_lo, bias, *rest):
    x1_ref, xm_ref, eid_ref, gate_ref = rest[-4:]
    lo = hf0[...] + hb0[...]
    hi = hf1[...] + hb1[...]
    hsum = jnp.concatenate([half[:, h * LANES:(h + 1) * LANES] for h in range(ML_H) for half in (lo, hi)], axis=-1)
    o = _head_rms(hsum, gg[...], ML_H, ML_DV) * _sigmoid_tanh(og[...])
    y = jnp.dot(o.astype(BF16), wout[...], preferred_element_type=F32)
    _post(y, x, m, lng, lnb, (w_hi, w_lo, bias), x1_ref, xm_ref, eid_ref, gate_ref)


def _router_weights(w_group, b_group, w_expert, b_expert):
    n = N_GROUPS + N_EXP
    wp = jnp.zeros((D, LANES), F32).at[:, :N_GROUPS].set(w_group).at[:, N_GROUPS:n].set(w_expert)
    bias = jnp.zeros((1, LANES), F32).at[0, :N_GROUPS].set(b_group).at[0, N_GROUPS:n].set(b_expert)
    w_hi, w_lo = _split2(wp)
    return w_hi, w_lo, bias


def _const_spec(shape):
    nd = len(shape)
    return pl.BlockSpec(shape, lambda *_: (0,) * nd)


def out_block(kind, acts, x, mvec, gg, wout, lng, lnb, rw, *, n_tok_all, row_off, carry=None, tm=256):
    B, T, _ = x.shape
    tm = min(tm, T)
    nt = T // tm
    tok = lambda width: pl.BlockSpec((None, tm, width), lambda b, i: (b, i, 0))
    act = lambda width, cb=0: pl.BlockSpec((None, tm, width), lambda b, i: (b, i, cb))
    if kind == "even":
        of, ob, P, hf, hb = acts
        a_in = [of, ob, P, hf, hb, P]
        a_specs = [act(GLA_V), act(GLA_V), act(GLA_V, 2), act(LRU_W), act(LRU_W), act(LRU_W, 4)]
        body = _out_even_kernel
    else:
        halves, P = acts
        a_in = list(halves) + [P]
        a_specs = [act(ML_H * LANES)] * 4 + [act(ML_V, 2)]
        body = _out_odd_kernel
    w_in = [mvec, gg.reshape(1, -1), wout, lng.reshape(1, D), lnb.reshape(1, D)] + list(rw)
    w_specs = [pl.BlockSpec((None, SUBLANES, D), lambda b, i: (b, 0, 0))] + [_const_spec(a.shape) for a in w_in[1:]]
    rb = row_off // tm
    blk_idx = lambda b, i: rb + b * nt + i
    route = pl.BlockSpec((SUBLANES, tm), lambda b, i: (0, blk_idx(b, i)))
    out_specs = [tok(D), pl.BlockSpec((tm * TOK_ROWS, LANES), lambda b, i: (blk_idx(b, i), 0)), route, route]
    out_shape = [jax.ShapeDtypeStruct(x.shape, F32), jax.ShapeDtypeStruct((n_tok_all * TOK_ROWS, LANES), F32),
                 jax.ShapeDtypeStruct((SUBLANES, n_tok_all), jnp.int32), jax.ShapeDtypeStruct((SUBLANES, n_tok_all), F32)]
    c_in, c_specs, aliases = [], [], {}
    if carry is None and n_tok_all != B * T:
        carry = tuple(jnp.zeros(s.shape, s.dtype) for s in out_shape[1:])
    if carry is not None:
        c_in = list(carry)
        c_specs = [pl.BlockSpec(memory_space=pl.ANY)] * 3
        base = len(a_in) + 1 + len(w_in)
        aliases = {base: 1, base + 1: 2, base + 2: 3}
    x1, xm, eid, gate = pl.pallas_call(
        body,
        grid=(B, nt),
        in_specs=a_specs + [tok(D)] + w_specs + c_specs,
        out_specs=out_specs,
        out_shape=out_shape,
        input_output_aliases=aliases,
        compiler_params=_params(2),
        name="out_" + kind,
    )(*a_in, x, *w_in, *c_in)
    return x1, xm, eid, gate


def _expert_kernel(be_ref, src_c, src_n, dst_p, dst_c, g_ref, wg, wu, wd, xm_hbm, y_hbm, xbuf, ybuf, gs, ss,
                   *, nb, n_slots):
    del be_ref
    n = pl.program_id(0)
    blk, tr, tp = MOE_BLK, TOK_ROWS, TOK_PITCH

    def gather(idx_ref, slot):
        for r in range(blk):
            row = pl.multiple_of(idx_ref[0, r], tr)
            pltpu.make_async_copy(xm_hbm.at[pl.ds(row, tr)], xbuf.at[slot, pl.ds(r * tp, tr)], gs.at[slot]).start()

    def scatter(idx_ref, slot):
        for r in range(blk):
            row = pl.multiple_of(idx_ref[0, r], tr)
            pltpu.make_async_copy(ybuf.at[slot, pl.ds(r * tp, tr)], y_hbm.at[pl.ds(row, tr)], ss.at[slot]).start()

    def wait_gather(slot):
        pltpu.make_async_copy(xm_hbm.at[pl.ds(0, blk * tr)], xbuf.at[slot, pl.ds(0, blk * tr)], gs.at[slot]).wait()

    def wait_scatter(slot):
        pltpu.make_async_copy(ybuf.at[slot, pl.ds(0, blk * tr)], y_hbm.at[pl.ds(0, blk * tr)], ss.at[slot]).wait()

    def mlp(slot):
        xs, ys = xbuf.at[slot], ybuf.at[slot]
        xb = jnp.concatenate([xs[pl.ds(j, blk, stride=tp), :] for j in range(tr)], axis=1).astype(BF16)
        hg = jnp.dot(xb, wg[...].astype(BF16), preferred_element_type=F32)
        hu = jnp.dot(xb, wu[...].astype(BF16), preferred_element_type=F32)
        h = (hg * _sigmoid_tanh(hg) * hu).astype(BF16)
        y = jnp.dot(h, wd[...].astype(BF16), preferred_element_type=F32) * g_ref[...]
        for j in range(tr):
            ys[pl.ds(j, blk, stride=tp), :] = y[:, j * LANES:(j + 1) * LANES]

    s = lax.rem(n, 2)

    @pl.when(n == 0)
    def _():
        ybuf[...] = jnp.zeros_like(ybuf)
        pltpu.make_async_copy(ybuf.at[0, pl.ds(0, blk * tr)], y_hbm.at[pl.ds((n_slots + blk) * tr, blk * tr)],
                              ss.at[0]).start()
        gather(src_c, 0)

    wait_gather(s)
    wait_scatter(s)
    gather(src_n, 1 - s)
    scatter(dst_p, 1 - s)
    mlp(s)

    @pl.when(n == nb - 1)
    def _():
        scatter(dst_c, s)
        wait_gather(1 - s)
        wait_scatter(1 - s)
        wait_scatter(s)


def moe_experts(xm, block_e, slot_src, slot_dst, slot_gate, wg, wu, wd, layer):
    nb, blk = slot_src.shape
    n_slots = nb * blk
    src3 = (slot_src * TOK_ROWS).reshape(nb, 1, blk)
    spare = (n_slots + jnp.arange(blk, dtype=jnp.int32)).reshape(1, 1, blk)
    dst3 = jnp.concatenate([spare, slot_dst.reshape(nb, 1, blk)], axis=0) * TOK_ROWS
    gate3 = slot_gate.reshape(nb, blk, 1)
    smem_blk = lambda f: pl.BlockSpec((None, 1, blk), f, memory_space=pltpu.SMEM)
    grid_spec = pltpu.PrefetchScalarGridSpec(
        num_scalar_prefetch=1,
        grid=(nb,),
        in_specs=[
            smem_blk(lambda n, be: (n, 0, 0)),
            smem_blk(lambda n, be: (jnp.minimum(n + 1, nb - 1), 0, 0)),
            smem_blk(lambda n, be: (n, 0, 0)),
            smem_blk(lambda n, be: (n + 1, 0, 0)),
            pl.BlockSpec((None, blk, 1), lambda n, be: (n, 0, 0)),
            pl.BlockSpec((None, None, D, D_EXP), lambda n, be: (layer, be[n], 0, 0)),
            pl.BlockSpec((None, None, D, D_EXP), lambda n, be: (layer, be[n], 0, 0)),
            pl.BlockSpec((None, None, D_EXP, D), lambda n, be: (layer, be[n], 0, 0)),
            pl.BlockSpec(memory_space=pl.ANY),
        ],
        out_specs=pl.BlockSpec(memory_space=pl.ANY),
        scratch_shapes=[
            pltpu.VMEM((2, blk * TOK_PITCH, LANES), F32),
            pltpu.VMEM((2, blk * TOK_PITCH, LANES), F32),
            pltpu.SemaphoreType.DMA((2,)),
            pltpu.SemaphoreType.DMA((2,)),
        ],
    )
    return pl.pallas_call(
        functools.partial(_expert_kernel, nb=nb, n_slots=n_slots),
        grid_spec=grid_spec,
        out_shape=jax.ShapeDtypeStruct(((n_slots + 2 * blk) * TOK_ROWS, LANES), F32),
        compiler_params=_params(1),
        name="moe_experts",
    )(block_e, src3, src3, dst3, dst3, gate3, wg, wu, wd, xm)


def moe_tables(eid, gate, n_tok):
    blk = MOE_BLK
    n_assign = 2 * n_tok
    flat_e = eid[:2].reshape(-1)
    flat_g = gate[:2].reshape(-1)
    counts = jnp.sum((flat_e[:, None] == jnp.arange(N_EXP, dtype=jnp.int32)[None, :]).astype(jnp.int32), axis=0)
    padded = (counts + blk - 1) // blk * blk
    pad_end = jnp.cumsum(padded)
    pad_start = pad_end - padded
    cnt_start = jnp.cumsum(counts) - counts
    order = jnp.argsort(flat_e).astype(jnp.int32)
    nb = -(-(n_assign + N_EXP * (blk - 1)) // blk)
    bstart = jnp.arange(nb, dtype=jnp.int32) * blk
    block_e = jnp.minimum(jnp.sum((pad_end[None, :] <= bstart[:, None]).astype(jnp.int32), axis=1), N_EXP - 1)
    lane = jnp.arange(blk, dtype=jnp.int32)[None, :]
    j = (bstart - pad_start[block_e])[:, None] + lane
    cnt_b = counts[block_e][:, None]
    valid = j < cnt_b
    n_valid_before = cnt_start[block_e][:, None] + jnp.minimum(j, cnt_b)
    a = order[jnp.clip(n_valid_before, 0, n_assign - 1)]
    slot = bstart[:, None] + lane
    slot_src = jnp.where(valid, jnp.where(a >= n_tok, a - n_tok, a), 0).astype(jnp.int32)
    slot_dst = jnp.where(valid, a, n_assign + slot - n_valid_before).astype(jnp.int32)
    slot_gate = jnp.where(valid, flat_g[a], 0.0)
    return block_e.astype(jnp.int32), slot_src, slot_dst, slot_gate


def _combine_kernel(x_ref, y0_ref, y1_ref, m_ref, g_ref, b_ref, o_ref):
    tm = x_ref.shape[0]
    y = jnp.concatenate([y0_ref[pl.ds(j, tm, stride=TOK_ROWS), :] + y1_ref[pl.ds(j, tm, stride=TOK_ROWS), :]
                         for j in range(TOK_ROWS)], axis=1)
    o_ref[...] = _layer_norm(ALPHA * x_ref[...] + m_ref[5:6, :] * y, g_ref[...], b_ref[...])


def moe_combine(x1, Y, mvec, lng, lnb, *, n_tok, row_off, tm=512):
    B, T, _ = x1.shape
    tm = min(tm, T)
    nt = T // tm
    rb = row_off // tm
    kb = n_tok // tm
    return pl.pallas_call(
        _combine_kernel,
        grid=(B, nt),
        in_specs=[
            pl.BlockSpec((None, tm, D), lambda b, i: (b, i, 0)),
            pl.BlockSpec((tm * TOK_ROWS, LANES), lambda b, i: (rb + b * nt + i, 0)),
            pl.BlockSpec((tm * TOK_ROWS, LANES), lambda b, i: (kb + rb + b * nt + i, 0)),
            pl.BlockSpec((None, SUBLANES, D), lambda b, i: (b, 0, 0)),
            _const_spec((1, D)), _const_spec((1, D)),
        ],
        out_specs=pl.BlockSpec((None, tm, D), lambda b, i: (b, i, 0)),
        out_shape=jax.ShapeDtypeStruct((B, T, D), F32),
        compiler_params=_params(2),
        name="moe_combine",
    )(x1, Y, Y, mvec, lng.reshape(1, D), lnb.reshape(1, D))


def _even_w_in(w):
    g0 = GLA_QK * 2 + GLA_V * 2
    g1 = g0 + 2 * GLA_RANK
    pad = jnp.zeros((D, EV_NP - w.shape[1]), w.dtype)
    return jnp.concatenate([w[:, :g0], w[:, g1:], w[:, g0:g1], pad], axis=1).astype(BF16)


def _odd_w_in(w):
    return jnp.concatenate([w, jnp.zeros((D, OD_NP - w.shape[1]), w.dtype)], axis=1).astype(BF16)


def _lru_weights(w_r, b_r, w_i, b_i):
    wdir = jnp.concatenate([w_r, w_i], axis=-1).astype(BF16)
    blk = lambda b: b.reshape(2, LRU_NB, 1, LRU_BW)
    bdir = jnp.concatenate([blk(b_r), blk(b_i)], axis=-1)
    return wdir, bdir


def _gla_gate_weights(w_a2, b_a):
    wa = jnp.zeros((2, LANES, GLA_QK), F32)
    wa = wa.at[0, :GLA_RANK].set(w_a2[0]).at[1, GLA_RANK:2 * GLA_RANK].set(w_a2[1])
    w_hi, w_lo = _split2(wa)
    return jnp.concatenate([w_hi, w_lo, w_hi], axis=1), b_a.reshape(2, 1, GLA_QK)


def _mlstm_gate_table(P, n_col):
    B, T, _ = P.shape
    g = P[:, :, OD_GATE_COL:OD_GATE_COL + 4 * ML_H].reshape(B, T // n_col, n_col, 4, ML_H)
    g = jnp.transpose(g, (0, 4, 2, 3, 1))
    return jnp.concatenate([g, jnp.zeros_like(g)], axis=3)


def _mvec(mods_l, rows):
    m6 = jnp.stack([mods_l[r] for r in rows]).reshape(len(rows), 6, D)
    return jnp.concatenate([m6, jnp.zeros((len(rows), 2, D), F32)], axis=1)


def kernel(x, c, ctx, c_ctx, w_mod, b_mod, ln_g, ln_b, ev_w_in, gla_w_a2, gla_b_a, gla_norm, lru_conv_w, lru_conv_b, lru_w_r, lru_b_r, lru_w_i, lru_b_i, lru_lam, ev_w_out, od_w_in, mlstm_b_gate, mlstm_norm, od_w_out, moe_w_group, moe_b_group, moe_w_expert, moe_b_expert, moe_w_gate, moe_w_up, moe_w_down):
    B, S, _ = x.shape
    T_ctx = ctx.shape[1]
    cond8 = jnp.zeros((SUBLANES, D), F32).at[:B].set(c).at[B].set(c_ctx)
    mods = cond_mod(cond8, w_mod, b_mod)
    h_ctx = ctx
    for l in range(DEPTH):
        j = l // 2
        last = l == DEPTH - 1
        m_lat = _mvec(mods[l], list(range(B)))
        m_ctx = _mvec(mods[l], [B] * B)
        rw = _router_weights(moe_w_group[l], moe_b_group[l], moe_w_expert[l], moe_b_expert[l])
        n_tok = B * S if last else B * (S + T_ctx)
        if l % 2 == 0:
            w_in = _even_w_in(ev_w_in[j])
            wa, ba = _gla_gate_weights(gla_w_a2[j], gla_b_a[j])
            wdir, bdir = _lru_weights(lru_w_r[j], lru_b_r[j], lru_w_i[j], lru_b_i[j])
            w_out = ev_w_out[j].astype(BF16)

            def mix_even(xin, mv, s0, h0):
                P = inproj(xin, mv, w_in, tn=EV_TN)
                of, ob, sfin = gla_scan(P, wa, ba, s0)
                hf, hb, hfin = lru_mix(P, lru_conv_w[j], lru_conv_b[j], wdir, bdir, lru_lam[j], h0)
                return (of, ob, P, hf, hb), sfin, hfin

            s0 = jnp.zeros((B, 2, GLA_H, GLA_DV, GLA_DK), F32)
            h0 = jnp.zeros((B, 2, SUBLANES, LRU_W), F32)
            acts_c, s_c, h_c = mix_even(h_ctx, m_ctx, s0, h0)
            acts_l, _, _ = mix_even(x, m_lat, s_c, h_c)
            carry = None
            if not last:
                hc1, *carry = out_block("even", acts_c, h_ctx, m_ctx, gla_norm[j], w_out, ln_g[l, 0], ln_b[l, 0], rw,
                                        n_tok_all=n_tok, row_off=B * S)
            x1, xm, eid, gate = out_block("even", acts_l, x, m_lat, gla_norm[j], w_out, ln_g[l, 0], ln_b[l, 0], rw,
                                          n_tok_all=n_tok, row_off=0, carry=carry)
        else:
            w_in = _odd_w_in(od_w_in[j])
            w_out = od_w_out[j].astype(BF16)
            bgate = jnp.concatenate([mlstm_b_gate[j].T, jnp.zeros((ML_H, 4), F32)], axis=1).reshape(ML_H, SUBLANES, 1)

            def mix_odd(xin, mv, c0, m0, n_col):
                P = inproj(xin, mv, w_in, tn=OD_TN)
                gt = _mlstm_gate_table(P, n_col)
                halves, cfin, mfin = mlstm_scan(P, gt, bgate, c0, m0, n_col=n_col)
                return (halves, P), cfin, mfin

            c0 = jnp.zeros((B, 2, ML_H, ML_DK, ML_AUG), F32)
            m0 = jnp.full((B, 2, ML_H, SUBLANES, LANES), M_INIT, F32)
            acts_c, c_c, m_c = mix_odd(h_ctx, m_ctx, c0, m0, 1)
            acts_l, _, _ = mix_odd(x, m_lat, c_c, m_c, GRID_W)
            carry = None
            if not last:
                hc1, *carry = out_block("odd", acts_c, h_ctx, m_ctx, mlstm_norm[j], w_out, ln_g[l, 0], ln_b[l, 0], rw,
                                        n_tok_all=n_tok, row_off=B * S)
            x1, xm, eid, gate = out_block("odd", acts_l, x, m_lat, mlstm_norm[j], w_out, ln_g[l, 0], ln_b[l, 0], rw,
                                          n_tok_all=n_tok, row_off=0, carry=carry)
        block_e, slot_src, slot_dst, slot_gate = moe_tables(eid, gate, n_tok)
        Y = moe_experts(xm, block_e, slot_src, slot_dst, slot_gate, moe_w_gate, moe_w_up, moe_w_down, l)
        x = moe_combine(x1, Y, m_lat, ln_g[l, 1], ln_b[l, 1], n_tok=n_tok, row_off=0)
        if not last:
            h_ctx = moe_combine(hc1, Y, m_ctx, ln_g[l, 1], ln_b[l, 1], n_tok=n_tok, row_off=B * S)
    return x
```

```python
import functools

import jax
import jax.numpy as jnp
from jax import lax
from jax.experimental import pallas as pl
from jax.experimental.pallas import tpu as pltpu

F32 = jnp.float32
BF16 = jnp.bfloat16

D = 2048
DEPTH = 2
GRID_W = 64
CHUNK = 64
GLA_H, GLA_DK, GLA_DV, GLA_RANK, GLA_TAU = 4, 128, 256, 16, 16.0
GLA_QK, GLA_V = GLA_H * GLA_DK, GLA_H * GLA_DV
LRU_W, LRU_NB, LRU_C, CONV_W = 1024, 8, 8.0, 4
LRU_BW = LRU_W // LRU_NB
ML_H, ML_DK, ML_DV = 8, 128, 256
ML_QK, ML_V = ML_H * ML_DK, ML_H * ML_DV
M_INIT = -1e30
N_GROUPS, EPG, N_EXP, D_EXP = 4, 8, 32, 512
ALPHA = (2 * DEPTH) ** 0.25
EPS = 1e-5

LANES = 128
SUBLANES = 8
VMEM_LIMIT = 56 * 1024 * 1024

EV_NP = 5376
EV_TN = 1792
EV_GATE_COL = 5120
OD_NP = 6400
OD_TN = 1280
OD_GATE_COL = 6144
MOE_BLK = 256
TOK_ROWS = D // LANES
TOK_PITCH = 24


def _params(n_axes):
    return pltpu.CompilerParams(dimension_semantics=("arbitrary",) * n_axes, vmem_limit_bytes=VMEM_LIMIT)


def _sigmoid(x):
    return 1.0 / (1.0 + jnp.exp(-x))


def _sigmoid_tanh(x):
    return 0.5 * jnp.tanh(0.5 * x) + 0.5


def _silu(x):
    return x * _sigmoid(x)


def _log_sigmoid(x):
    return jnp.minimum(x, 0.0) - jnp.log1p(jnp.exp(-jnp.abs(x)))


def _gelu_tanh(x):
    return 0.5 * x * (1.0 + jnp.tanh(0.7978845608028654 * (x + 0.044715 * (x * x * x))))


def _bdot(a, b):
    return jnp.dot(a.astype(BF16), b.astype(BF16), preferred_element_type=F32)


def _bdot_nt(a, b):
    return lax.dot_general(a.astype(BF16), b.astype(BF16), (((1,), (1,)), ((), ())), preferred_element_type=F32)


def _bdot_tn(a, b):
    return lax.dot_general(a.astype(BF16), b.astype(BF16), (((0,), (0,)), ((), ())), preferred_element_type=F32)


def _split2(x):
    hi = x.astype(BF16)
    return hi, (x - hi.astype(F32)).astype(BF16)


def _split3(x):
    hi = x.astype(BF16)
    r = x - hi.astype(F32)
    mid = r.astype(BF16)
    return hi, mid, (r - mid.astype(F32)).astype(BF16)


def _mod_kernel(c_ref, w_ref, b_ref, o_ref):
    o_ref[...] = _bdot(_silu(c_ref[...]), w_ref[...]) + b_ref[...]


def cond_mod(cond8, w_mod, b_mod):
    nl, _, n6 = w_mod.shape
    tn = 1024
    return pl.pallas_call(
        _mod_kernel,
        grid=(nl, n6 // tn),
        in_specs=[
            pl.BlockSpec((SUBLANES, D), lambda l, j: (0, 0)),
            pl.BlockSpec((None, D, tn), lambda l, j: (l, 0, j)),
            pl.BlockSpec((None, 1, tn), lambda l, j: (l, 0, j)),
        ],
        out_specs=pl.BlockSpec((None, SUBLANES, tn), lambda l, j: (l, 0, j)),
        out_shape=jax.ShapeDtypeStruct((nl, SUBLANES, n6), F32),
        compiler_params=_params(2),
        name="cond_mod",
    )(cond8, w_mod, b_mod.reshape(nl, 1, n6))


def _inproj_kernel(x_ref, m_ref, w_ref, o_ref, u_ref):
    @pl.when(pl.program_id(2) == 0)
    def _():
        u_ref[...] = (x_ref[...] * (1.0 + m_ref[1:2, :]) + m_ref[0:1, :]).astype(BF16)

    o_ref[...] = jnp.dot(u_ref[...], w_ref[...], preferred_element_type=F32)


def inproj(x3, mvec, w, *, tn, tm=1024):
    B, T, _ = x3.shape
    Np = w.shape[1]
    tm = min(tm, T)
    return pl.pallas_call(
        _inproj_kernel,
        grid=(B, T // tm, Np // tn),
        in_specs=[
            pl.BlockSpec((None, tm, D), lambda b, i, j: (b, i, 0)),
            pl.BlockSpec((None, SUBLANES, D), lambda b, i, j: (b, 0, 0)),
            pl.BlockSpec((D, tn), lambda b, i, j: (0, j)),
        ],
        out_specs=pl.BlockSpec((None, tm, tn), lambda b, i, j: (b, i, j)),
        out_shape=jax.ShapeDtypeStruct((B, T, Np), F32),
        scratch_shapes=[pltpu.VMEM((tm, D), BF16)],
        compiler_params=_params(3),
        name="inproj",
    )(x3, mvec, w)


def _gla_kernel(qf, kf, vf, gf, qb, kb, vb, gb, wa_ref, ba_ref, s0_ref, of_ref, ob_ref, sfin_ref, st_ref, *, tb, ns):
    s = pl.program_id(1)

    @pl.when(s == 0)
    def _():
        st_ref[...] = s0_ref[...]

    L = CHUNK
    row = lax.broadcasted_iota(jnp.int32, (L, L), 0)
    col = lax.broadcasted_iota(jnp.int32, (L, L), 1)
    row3 = lax.broadcasted_iota(jnp.int32, (L, 3 * L), 0)
    col3 = lax.rem(lax.broadcasted_iota(jnp.int32, (L, 3 * L), 1), L)
    scale = GLA_DK ** -0.5
    for d, (q_ref, k_ref, v_ref, g_ref, o_ref) in enumerate(((qf, kf, vf, gf, of_ref), (qb, kb, vb, gb, ob_ref))):
        mask = (row >= col) if d == 0 else (row <= col)
        m3 = jnp.where((row3 >= col3) if d == 0 else (row3 <= col3), 1.0, 0.0).astype(BF16)
        x_hi, x_lo = _split2(g_ref[...])
        gpre = jnp.dot(jnp.concatenate([x_hi, x_hi, x_lo], axis=1), wa_ref[d], preferred_element_type=F32)
        glog = _log_sigmoid(gpre + ba_ref[d]) * (1.0 / GLA_TAU)
        nchunk = tb // L
        for c in (range(nchunk) if d == 0 else range(nchunk - 1, -1, -1)):
            r0 = c * L
            g1, g2, g3 = _split3(glog[r0:r0 + L, :])
            b_all = jnp.dot(m3, jnp.concatenate([g1, g2, g3], axis=0), preferred_element_type=F32)
            for h in range(GLA_H):
                q = q_ref[r0:r0 + L, h * GLA_DK:(h + 1) * GLA_DK]
                k = k_ref[r0:r0 + L, h * GLA_DK:(h + 1) * GLA_DK]
                v = v_ref[r0:r0 + L, h * GLA_DV:(h + 1) * GLA_DV].astype(BF16)
                b = b_all[:, h * GLA_DK:(h + 1) * GLA_DK]
                bl = b[L - 1:L, :] if d == 0 else b[0:1, :]
                qd = (q * (jnp.exp(b) * scale)).astype(BF16)
                ki = k * jnp.exp(-b)
                ke = k * jnp.exp(bl - b)
                att = jnp.where(mask, _bdot_nt(qd, ki), 0.0)
                st = st_ref[d, h]
                o = _bdot(att, v) + _bdot_nt(qd, st)
                o_ref[r0:r0 + L, h * GLA_DV:(h + 1) * GLA_DV] = o
                st_ref[d, h] = st * jnp.exp(bl) + _bdot_tn(v, ke)

    @pl.when(s == ns - 1)
    def _():
        sfin_ref[...] = st_ref[...]


def gla_scan(P, wa, ba, s0, *, tb=512):
    B, T, _ = P.shape
    tb = min(tb, T)
    ns = T // tb
    fwd = lambda b, s: s
    bwd = lambda b, s: ns - 1 - s

    def specs(pos):
        return [
            pl.BlockSpec((None, tb, GLA_QK), lambda b, s: (b, pos(b, s), 0)),
            pl.BlockSpec((None, tb, GLA_QK), lambda b, s: (b, pos(b, s), 1)),
            pl.BlockSpec((None, tb, GLA_V), lambda b, s: (b, pos(b, s), 1)),
            pl.BlockSpec((None, tb, LANES), lambda b, s: (b, pos(b, s), EV_GATE_COL // LANES)),
        ]

    st_shape = (2, GLA_H, GLA_DV, GLA_DK)
    st_spec = pl.BlockSpec((None,) + st_shape, lambda b, s: (b, 0, 0, 0, 0))
    return pl.pallas_call(
        functools.partial(_gla_kernel, tb=tb, ns=ns),
        grid=(B, ns),
        in_specs=specs(fwd) + specs(bwd) + [
            pl.BlockSpec((2, 3 * LANES, GLA_QK), lambda b, s: (0, 0, 0)),
            pl.BlockSpec((2, 1, GLA_QK), lambda b, s: (0, 0, 0)),
            st_spec,
        ],
        out_specs=[
            pl.BlockSpec((None, tb, GLA_V), lambda b, s: (b, s, 0)),
            pl.BlockSpec((None, tb, GLA_V), lambda b, s: (b, ns - 1 - s, 0)),
            st_spec,
        ],
        out_shape=[
            jax.ShapeDtypeStruct((B, T, GLA_V), F32),
            jax.ShapeDtypeStruct((B, T, GLA_V), F32),
            jax.ShapeDtypeStruct((B,) + st_shape, F32),
        ],
        scratch_shapes=[pltpu.VMEM(st_shape, F32)],
        compiler_params=_params(2),
        name="gla_scan",
    )(P, P, P, P, P, P, P, P, wa, ba, s0)


def _lru_kernel(xf, pf, nf, xb, pb, nb_, cw_ref, cb_ref, w_ref, bias_ref, lam_ref, h0_ref,
                hf_ref, hb_ref, hfin_ref, a_s, u_s, h_s, *, tm, nt):
    i = pl.program_id(1)

    @pl.when(i == 0)
    def _():
        h_s[...] = h0_ref[...]

    cw = cw_ref[...]
    off = SUBLANES - CONV_W // 2
    for d, (x_ref, p_ref, n_ref) in enumerate(((xf, pf, nf), (xb, pb, nb_))):
        ti = i if d == 0 else nt - 1 - i
        prev = jnp.where(ti > 0, p_ref[...], 0.0)
        nxt = jnp.where(ti < nt - 1, n_ref[...], 0.0)
        ext = jnp.concatenate([prev, x_ref[...], nxt], axis=0)
        xc = cb_ref[...] + (ext[off:off + tm] * cw[0:1] + ext[off + 1:off + 1 + tm] * cw[1:2]
                            + ext[off + 2:off + 2 + tm] * cw[2:3] + ext[off + 3:off + 3 + tm] * cw[3:4])
        for n in range(LRU_NB):
            cs = slice(n * LRU_BW, (n + 1) * LRU_BW)
            xn = xc[:, cs]
            z = _bdot(xn, w_ref[d, n]) + bias_ref[d, n]
            r = _sigmoid_tanh(z[:, :LRU_BW])
            ig = _sigmoid_tanh(z[:, LRU_BW:])
            a = jnp.exp((LRU_C * r) * _log_sigmoid(lam_ref[d:d + 1, cs]))
            a_s[d, :, cs] = a
            u_s[d, :, cs] = jnp.sqrt(1.0 - a * a) * (ig * xn)

    sub = lax.broadcasted_iota(jnp.int32, (SUBLANES, LRU_W), 0)

    def group(g, carry):
        hf, hb = carry
        bf = pl.multiple_of(g * SUBLANES, SUBLANES)
        bb = pl.multiple_of(tm - SUBLANES - g * SUBLANES, SUBLANES)
        a_f, u_f = a_s[0, pl.ds(bf, SUBLANES), :], u_s[0, pl.ds(bf, SUBLANES), :]
        a_b, u_b = a_s[1, pl.ds(bb, SUBLANES), :], u_s[1, pl.ds(bb, SUBLANES), :]
        for j in range(SUBLANES):
            hf = jnp.where(sub == j, a_f * pltpu.roll(hf, 1, axis=0) + u_f, hf)
            jb = SUBLANES - 1 - j
            hb = jnp.where(sub == jb, a_b * pltpu.roll(hb, SUBLANES - 1, axis=0) + u_b, hb)
        hf_ref[pl.ds(bf, SUBLANES), :] = hf
        hb_ref[pl.ds(bb, SUBLANES), :] = hb
        return hf, hb

    hf, hb = lax.fori_loop(0, tm // SUBLANES, group, (h_s[0], h_s[1]))
    h_s[0] = hf
    h_s[1] = hb

    @pl.when(i == nt - 1)
    def _():
        hfin_ref[...] = h_s[...]


def lru_mix(P, conv_w, conv_b, wdir, bdir, lam, h0, *, tm=512):
    B, T, _ = P.shape
    tm = min(tm, T)
    nt = T // tm
    xcol = 3
    r8 = tm // SUBLANES
    n8 = T // SUBLANES
    fwd = lambda i: i
    bwd = lambda i: nt - 1 - i

    def xspecs(pos):
        return [
            pl.BlockSpec((None, tm, LRU_W), lambda b, i: (b, pos(i), xcol)),
            pl.BlockSpec((None, SUBLANES, LRU_W), lambda b, i: (b, jnp.maximum(pos(i) * r8 - 1, 0), xcol)),
            pl.BlockSpec((None, SUBLANES, LRU_W), lambda b, i: (b, jnp.minimum((pos(i) + 1) * r8, n8 - 1), xcol)),
        ]

    hspec = pl.BlockSpec((None, 2, SUBLANES, LRU_W), lambda b, i: (b, 0, 0, 0))
    seq = jax.ShapeDtypeStruct((B, T, LRU_W), F32)
    return pl.pallas_call(
        functools.partial(_lru_kernel, tm=tm, nt=nt),
        grid=(B, nt),
        in_specs=xspecs(fwd) + xspecs(bwd) + [
            _const_spec((CONV_W, LRU_W)), _const_spec((1, LRU_W)),
            _const_spec((2, LRU_NB, LRU_BW, 2 * LRU_BW)), _const_spec((2, LRU_NB, 1, 2 * LRU_BW)),
            _const_spec((2, LRU_W)), hspec,
        ],
        out_specs=[
            pl.BlockSpec((None, tm, LRU_W), lambda b, i: (b, i, 0)),
            pl.BlockSpec((None, tm, LRU_W), lambda b, i: (b, nt - 1 - i, 0)),
            hspec,
        ],
        out_shape=[seq, seq, jax.ShapeDtypeStruct((B, 2, SUBLANES, LRU_W), F32)],
        scratch_shapes=[pltpu.VMEM((2, tm, LRU_W), F32), pltpu.VMEM((2, tm, LRU_W), F32),
                        pltpu.VMEM((2, SUBLANES, LRU_W), F32)],
        compiler_params=_params(2),
        name="lru_mix",
    )(P, P, P, P, P, P, conv_w, conv_b.reshape(1, LRU_W), wdir, bdir, lam, h0)


ML_AUG = ML_DV + LANES
ML_L = 256


def _chunk_rows(ref, w):
    L, nwl, width = ref.shape
    flat = ref.reshape(L * nwl, width)
    return flat.at[pl.ds(w, L, stride=nwl), :] if nwl > 1 else flat


def _mlstm_kernel(qf, kf, vf0, vf1, gf, qb, kb, vb0, vb1, gb, bg_ref, c0_ref, m0_ref,
                  hf0_ref, hf1_ref, hb0_ref, hb1_ref, cfin_ref, mfin_ref, c_ref, m_ref, *, L, nwl, ns):
    s = pl.program_id(2)

    @pl.when(s == 0)
    def _():
        c_ref[...] = c0_ref[...]
        m_ref[...] = m0_ref[...]

    row = lax.broadcasted_iota(jnp.int32, (L, L), 0)
    col = lax.broadcasted_iota(jnp.int32, (L, L), 1)
    nr = nwl * SUBLANES
    sub = lax.rem(lax.broadcasted_iota(jnp.int32, (nr, L), 0), SUBLANES)
    lane = lax.broadcasted_iota(jnp.int32, (nr, L), 1)
    is_f = (sub == 1) | (sub == 3)
    ones_col = (lax.broadcasted_iota(jnp.int32, (L, LANES), 1) == 0).astype(BF16)
    kscale = ML_DK ** -0.5
    bias = jnp.concatenate([bg_ref[...]] * nwl, axis=0)

    chunk_rows = _chunk_rows

    def gate_vectors(g_ref, d):
        pre = g_ref[...].reshape(nr, L) + bias
        xg = jnp.where(is_f, _log_sigmoid(pre), pre)
        cs = xg
        k_ = 1
        while k_ < L:
            if d == 0:
                cs = cs + jnp.where(lane >= k_, pltpu.roll(cs, k_, axis=1), 0.0)
            else:
                cs = cs + jnp.where(lane < L - k_, pltpu.roll(cs, L - k_, axis=1), 0.0)
            k_ *= 2
        ri, rf = 2 * d, 2 * d + 1
        pad = jnp.zeros((LANES - nr, L), F32)
        cs_t = jnp.concatenate([cs, pad], axis=0).T
        xg_t = jnp.concatenate([xg, pad], axis=0).T
        out = []
        for w in range(nwl):
            jb, ji = SUBLANES * w + rf, SUBLANES * w + ri
            out.append((cs[jb:jb + 1, :], xg[ji:ji + 1, :], cs_t[:, jb:jb + 1], xg_t[:, ji:ji + 1]))
        return out

    vec = (gate_vectors(gf, 0), gate_vectors(gb, 1))
    state = [(m_ref[d][0:1, 0:1], c_ref[d]) for d in range(2)]
    work = [(0, w, qf, kf, (vf0, vf1), (hf0_ref, hf1_ref)) for w in range(nwl)]
    work_b = [(1, w, qb, kb, (vb0, vb1), (hb0_ref, hb1_ref)) for w in range(nwl - 1, -1, -1)]
    for pair in zip(work, work_b):
      for d, w, q_ref, k_ref, v_refs, h_refs in pair:
        mask = (row >= col) if d == 0 else (row <= col)
        b_row, i_row, b_col, i_col = vec[d][w]
        b_last = b_row[:, L - 1:L] if d == 0 else b_row[:, 0:1]
        dmat = jnp.where(mask, b_col + (i_row - b_row), -jnp.inf)
        m_loc = jnp.max(dmat, axis=1, keepdims=True)
        q = chunk_rows(q_ref, w)[...].astype(BF16)
        k = chunk_rows(k_ref, w)[...] * kscale
        vaug = jnp.concatenate([chunk_rows(v_refs[0], w)[...].astype(BF16), chunk_rows(v_refs[1], w)[...].astype(BF16),
                                ones_col], axis=1)
        sc = _bdot_nt(q, k) * jnp.exp(dmat - m_loc)
        n_loc = _bdot(sc, vaug)
        g_row = b_last - b_row + i_row
        m_k = jnp.max(g_row, axis=1, keepdims=True)
        kv_loc = _bdot_tn(jnp.exp(b_last - b_col + i_col - m_k) * k, vaug)
        m, caug = state[d]
        inter = b_col + m
        m_t = jnp.maximum(inter, m_loc)
        acc = jnp.exp(m_loc - m_t) * n_loc + jnp.exp(inter - m_t) * _bdot(q, caug)
        den = acc[:, ML_DV:ML_DV + 1]
        h = acc[:, :ML_DV] / jnp.maximum(jnp.abs(den), jnp.exp(-m_t))
        chunk_rows(h_refs[0], w)[...] = h[:, :LANES]
        chunk_rows(h_refs[1], w)[...] = h[:, LANES:]
        m_new = jnp.maximum(b_last + m, m_k)
        state[d] = (m_new, jnp.exp(b_last + m - m_new) * caug + jnp.exp(m_k - m_new) * kv_loc)

    for d in range(2):
        m_ref[d] = jnp.broadcast_to(state[d][0], (SUBLANES, LANES))
        c_ref[d] = state[d][1]

    @pl.when(s == ns - 1)
    def _():
        cfin_ref[...] = c_ref[...]
        mfin_ref[...] = m_ref[...]


def mlstm_scan(P, gt, bgate, c0, m0, *, n_col):
    B, T, Np = P.shape
    L = T // n_col
    nwl = min(n_col, SUBLANES)
    ns = n_col // nwl
    P5 = P.reshape(B, L, ns, nwl, Np)
    fwd = lambda s: s
    bwd = lambda s: ns - 1 - s

    def specs(pos):
        return [
            pl.BlockSpec((None, L, None, nwl, ML_DK), lambda b, h, s: (b, 0, pos(s), 0, h)),
            pl.BlockSpec((None, L, None, nwl, ML_DK), lambda b, h, s: (b, 0, pos(s), 0, ML_H + h)),
            pl.BlockSpec((None, L, None, nwl, LANES), lambda b, h, s: (b, 0, pos(s), 0, 2 * ML_H + 2 * h)),
            pl.BlockSpec((None, L, None, nwl, LANES), lambda b, h, s: (b, 0, pos(s), 0, 2 * ML_H + 2 * h + 1)),
            pl.BlockSpec((None, None, nwl, SUBLANES, L), lambda b, h, s: (b, h, pos(s), 0, 0)),
        ]

    cshape = (2, ML_DK, ML_AUG)
    mshape = (2, SUBLANES, LANES)
    cspec = pl.BlockSpec((None, 2, None, ML_DK, ML_AUG), lambda b, h, s: (b, 0, h, 0, 0))
    mspec = pl.BlockSpec((None, 2, None, SUBLANES, LANES), lambda b, h, s: (b, 0, h, 0, 0))
    seq = jax.ShapeDtypeStruct((B, L, ns, nwl, ML_H * LANES), F32)
    ospec = lambda pos: pl.BlockSpec((None, L, None, nwl, LANES), lambda b, h, s: (b, 0, pos(s), 0, h))
    hf0, hf1, hb0, hb1, cfin, mfin = pl.pallas_call(
        functools.partial(_mlstm_kernel, L=L, nwl=nwl, ns=ns),
        grid=(B, ML_H, ns),
        in_specs=specs(fwd) + specs(bwd) + [
            pl.BlockSpec((None, SUBLANES, 1), lambda b, h, s: (h, 0, 0)),
            cspec, mspec,
        ],
        out_specs=[ospec(fwd), ospec(fwd), ospec(bwd), ospec(bwd), cspec, mspec],
        out_shape=[
            seq, seq, seq, seq,
            jax.ShapeDtypeStruct((B, 2, ML_H, ML_DK, ML_AUG), F32),
            jax.ShapeDtypeStruct((B, 2, ML_H, SUBLANES, LANES), F32),
        ],
        scratch_shapes=[pltpu.VMEM(cshape, F32), pltpu.VMEM(mshape, F32)],
        compiler_params=_params(3),
        name="mlstm_scan",
    )(P5, P5, P5, P5, gt, P5, P5, P5, P5, gt, bgate, c0, m0)
    flat = lambda a: a.reshape(B, T, ML_H * LANES)
    return (flat(hf0), flat(hf1), flat(hb0), flat(hb1)), cfin, mfin


def _layer_norm(z, g, b):
    mu = jnp.mean(z, axis=-1, keepdims=True)
    zc = z - mu
    var = jnp.mean(zc * zc, axis=-1, keepdims=True)
    return zc * lax.rsqrt(var + EPS) * g + b


def _head_rms(o, g, n_heads, width):
    parts = []
    for h in range(n_heads):
        oh = o[:, h * width:(h + 1) * width]
        parts.append(oh * lax.rsqrt(jnp.mean(oh * oh, axis=-1, keepdims=True) + EPS) * g)
    return jnp.concatenate(parts, axis=-1)


def _post(y, x_ref, m_ref, lng_ref, lnb_ref, rw_refs, x1_ref, xm_ref, eid_ref, gate_ref):
    w_hi, w_lo, bias = rw_refs
    x1 = _layer_norm(ALPHA * x_ref[...] + m_ref[2:3, :] * y, lng_ref[...], lnb_ref[...])
    x1_ref[...] = x1
    xm = x1 * (1.0 + m_ref[4:5, :]) + m_ref[3:4, :]
    tm = xm.shape[0]
    for j in range(TOK_ROWS):
        xm_ref[pl.ds(j, tm, stride=TOK_ROWS), :] = xm[:, j * LANES:(j + 1) * LANES]
    hi, lo = _split2(xm)
    lg = (jnp.dot(hi, w_hi[...], preferred_element_type=F32) + jnp.dot(hi, w_lo[...], preferred_element_type=F32)
          + jnp.dot(lo, w_hi[...], preferred_element_type=F32)) + bias[...]
    lane = lax.broadcasted_iota(jnp.int32, lg.shape, 1)
    neg = -jnp.inf
    glm = jnp.where(lane < N_GROUPS, lg, neg)
    gmax = jnp.max(glm, axis=-1, keepdims=True)
    gidx = jnp.min(jnp.where(glm == gmax, lane, LANES), axis=-1, keepdims=True)
    p_g = 1.0 / jnp.sum(jnp.exp(glm - gmax), axis=-1, keepdims=True)
    e_lane = lane - N_GROUPS
    in_group = (lax.shift_right_arithmetic(e_lane, 3) == gidx) & (e_lane < N_EXP)
    elm = jnp.where(in_group, lg, neg)
    m1 = jnp.max(elm, axis=-1, keepdims=True)
    i1 = jnp.min(jnp.where(elm == m1, e_lane, LANES), axis=-1, keepdims=True)
    elm2 = jnp.where(e_lane == i1, neg, elm)
    m2 = jnp.max(elm2, axis=-1, keepdims=True)
    i2 = jnp.min(jnp.where(elm2 == m2, e_lane, LANES), axis=-1, keepdims=True)
    e2 = jnp.exp(m2 - m1)
    g1 = p_g / (1.0 + e2)
    g2 = g1 * e2
    eid_ref[...] = jnp.where(lane == 0, i1, jnp.where(lane == 1, i2, 0)).T[:SUBLANES, :]
    gate_ref[...] = jnp.where(lane == 0, g1, jnp.where(lane == 1, g2, 0.0)).T[:SUBLANES, :]


def _out_even_kernel(of, ob, r, hf, hb, xg, x, m, gg, wout, lng, lnb, w_hi, w_lo, bias, *rest):
    x1_ref, xm_ref, eid_ref, gate_ref = rest[-4:]
    rg = r[...]
    o = _head_rms(of[...] + ob[...], gg[...], GLA_H, GLA_DV) * (rg * _sigmoid_tanh(rg))
    y2 = (hf[...] + hb[...]) * _gelu_tanh(xg[...])
    mix = jnp.concatenate([o.astype(BF16), y2.astype(BF16)], axis=-1)
    y = jnp.dot(mix, wout[...], preferred_element_type=F32)
    _post(y, x, m, lng, lnb, (w_hi, w_lo, bias), x1_ref, xm_ref, eid_ref, gate_ref)


def _out_odd_kernel(hf0, hf1, hb0, hb1, og, x, m, gg, wout, lng, lnb, w_hi, w_lo, bias, *rest):
    x1_ref, xm_ref, eid_ref, gate_ref = rest[-4:]
    lo = hf0[...] + hb0[...]
    hi = hf1[...] + hb1[...]
    hsum = jnp.concatenate([half[:, h * LANES:(h + 1) * LANES] for h in range(ML_H) for half in (lo, hi)], axis=-1)
    o = _head_rms(hsum, gg[...], ML_H, ML_DV) * _sigmoid_tanh(og[...])
    y = jnp.dot(o.astype(BF16), wout[...], preferred_element_type=F32)
    _post(y, x, m, lng, lnb, (w_hi, w_lo, bias), x1_ref, xm_ref, eid_ref, gate_ref)


def _router_weights(w_group, b_group, w_expert, b_expert):
    n = N_GROUPS + N_EXP
    wp = jnp.zeros((D, LANES), F32).at[:, :N_GROUPS].set(w_group).at[:, N_GROUPS:n].set(w_expert)
    bias = jnp.zeros((1, LANES), F32).at[0, :N_GROUPS].set(b_group).at[0, N_GROUPS:n].set(b_expert)
    w_hi, w_lo = _split2(wp)
    return w_hi, w_lo, bias


def _const_spec(shape):
    nd = len(shape)
    return pl.BlockSpec(shape, lambda *_: (0,) * nd)


def out_block(kind, acts, x, mvec, gg, wout, lng, lnb, rw, *, n_tok_all, row_off, carry=None, tm=256):
    B, T, _ = x.shape
    tm = min(tm, T)
    nt = T // tm
    tok = lambda width: pl.BlockSpec((None, tm, width), lambda b, i: (b, i, 0))
    act = lambda width, cb=0: pl.BlockSpec((None, tm, width), lambda b, i: (b, i, cb))
    if kind == "even":
        of, ob, P, hf, hb = acts
        a_in = [of, ob, P, hf, hb, P]
        a_specs = [act(GLA_V), act(GLA_V), act(GLA_V, 2), act(LRU_W), act(LRU_W), act(LRU_W, 4)]
        body = _out_even_kernel
    else:
        halves, P = acts
        a_in = list(halves) + [P]
        a_specs = [act(ML_H * LANES)] * 4 + [act(ML_V, 2)]
        body = _out_odd_kernel
    w_in = [mvec, gg.reshape(1, -1), wout, lng.reshape(1, D), lnb.reshape(1, D)] + list(rw)
    w_specs = [pl.BlockSpec((None, SUBLANES, D), lambda b, i: (b, 0, 0))] + [_const_spec(a.shape) for a in w_in[1:]]
    rb = row_off // tm
    blk_idx = lambda b, i: rb + b * nt + i
    route = pl.BlockSpec((SUBLANES, tm), lambda b, i: (0, blk_idx(b, i)))
    out_specs = [tok(D), pl.BlockSpec((tm * TOK_ROWS, LANES), lambda b, i: (blk_idx(b, i), 0)), route, route]
    out_shape = [jax.ShapeDtypeStruct(x.shape, F32), jax.ShapeDtypeStruct((n_tok_all * TOK_ROWS, LANES), F32),
                 jax.ShapeDtypeStruct((SUBLANES, n_tok_all), jnp.int32), jax.ShapeDtypeStruct((SUBLANES, n_tok_all), F32)]
    c_in, c_specs, aliases = [], [], {}
    if carry is None and n_tok_all != B * T:
        carry = tuple(jnp.zeros(s.shape, s.dtype) for s in out_shape[1:])
    if carry is not None:
        c_in = list(carry)
        c_specs = [pl.BlockSpec(memory_space=pl.ANY)] * 3
        base = len(a_in) + 1 + len(w_in)
        aliases = {base: 1, base + 1: 2, base + 2: 3}
    x1, xm, eid, gate = pl.pallas_call(
        body,
        grid=(B, nt),
        in_specs=a_specs + [tok(D)] + w_specs + c_specs,
        out_specs=out_specs,
        out_shape=out_shape,
        input_output_aliases=aliases,
        compiler_params=_params(2),
        name="out_" + kind,
    )(*a_in, x, *w_in, *c_in)
    return x1, xm, eid, gate


def _expert_kernel(be_ref, src_c, src_n, dst_p, dst_c, g_ref, wg, wu, wd, xm_hbm, y_hbm, xbuf, ybuf, gs, ss,
                   *, nb, n_slots):
    del be_ref
    n = pl.program_id(0)
    blk, tr, tp = MOE_BLK, TOK_ROWS, TOK_PITCH

    def gather(idx_ref, slot):
        for r in range(blk):
            row = pl.multiple_of(idx_ref[0, r], tr)
            pltpu.make_async_copy(xm_hbm.at[pl.ds(row, tr)], xbuf.at[slot, pl.ds(r * tp, tr)], gs.at[slot]).start()

    def scatter(idx_ref, slot):
        for r in range(blk):
            row = pl.multiple_of(idx_ref[0, r], tr)
            pltpu.make_async_copy(ybuf.at[slot, pl.ds(r * tp, tr)], y_hbm.at[pl.ds(row, tr)], ss.at[slot]).start()

    def wait_gather(slot):
        pltpu.make_async_copy(xm_hbm.at[pl.ds(0, blk * tr)], xbuf.at[slot, pl.ds(0, blk * tr)], gs.at[slot]).wait()

    def wait_scatter(slot):
        pltpu.make_async_copy(ybuf.at[slot, pl.ds(0, blk * tr)], y_hbm.at[pl.ds(0, blk * tr)], ss.at[slot]).wait()

    def mlp(slot):
        xs, ys = xbuf.at[slot], ybuf.at[slot]
        xb = jnp.concatenate([xs[pl.ds(j, blk, stride=tp), :] for j in range(tr)], axis=1).astype(BF16)
        hg = jnp.dot(xb, wg[...].astype(BF16), preferred_element_type=F32)
        hu = jnp.dot(xb, wu[...].astype(BF16), preferred_element_type=F32)
        h = (hg * _sigmoid_tanh(hg) * hu).astype(BF16)
        y = jnp.dot(h, wd[...].astype(BF16), preferred_element_type=F32) * g_ref[...]
        for j in range(tr):
            ys[pl.ds(j, blk, stride=tp), :] = y[:, j * LANES:(j + 1) * LANES]

    s = lax.rem(n, 2)

    @pl.when(n == 0)
    def _():
        ybuf[...] = jnp.zeros_like(ybuf)
        pltpu.make_async_copy(ybuf.at[0, pl.ds(0, blk * tr)], y_hbm.at[pl.ds((n_slots + blk) * tr, blk * tr)],
                              ss.at[0]).start()
        gather(src_c, 0)

    wait_gather(s)
    wait_scatter(s)
    gather(src_n, 1 - s)
    scatter(dst_p, 1 - s)
    mlp(s)

    @pl.when(n == nb - 1)
    def _():
        scatter(dst_c, s)
        wait_gather(1 - s)
        wait_scatter(1 - s)
        wait_scatter(s)


def moe_experts(xm, block_e, slot_src, slot_dst, slot_gate, wg, wu, wd, layer):
    nb, blk = slot_src.shape
    n_slots = nb * blk
    src3 = (slot_src * TOK_ROWS).reshape(nb, 1, blk)
    spare = (n_slots + jnp.arange(blk, dtype=jnp.int32)).reshape(1, 1, blk)
    dst3 = jnp.concatenate([spare, slot_dst.reshape(nb, 1, blk)], axis=0) * TOK_ROWS
    gate3 = slot_gate.reshape(nb, blk, 1)
    smem_blk = lambda f: pl.BlockSpec((None, 1, blk), f, memory_space=pltpu.SMEM)
    grid_spec = pltpu.PrefetchScalarGridSpec(
        num_scalar_prefetch=1,
        grid=(nb,),
        in_specs=[
            smem_blk(lambda n, be: (n, 0, 0)),
            smem_blk(lambda n, be: (jnp.minimum(n + 1, nb - 1), 0, 0)),
            smem_blk(lambda n, be: (n, 0, 0)),
            smem_blk(lambda n, be: (n + 1, 0, 0)),
            pl.BlockSpec((None, blk, 1), lambda n, be: (n, 0, 0)),
            pl.BlockSpec((None, None, D, D_EXP), lambda n, be: (layer, be[n], 0, 0)),
            pl.BlockSpec((None, None, D, D_EXP), lambda n, be: (layer, be[n], 0, 0)),
            pl.BlockSpec((None, None, D_EXP, D), lambda n, be: (layer, be[n], 0, 0)),
            pl.BlockSpec(memory_space=pl.ANY),
        ],
        out_specs=pl.BlockSpec(memory_space=pl.ANY),
        scratch_shapes=[
            pltpu.VMEM((2, blk * TOK_PITCH, LANES), F32),
            pltpu.VMEM((2, blk * TOK_PITCH, LANES), F32),
            pltpu.SemaphoreType.DMA((2,)),
            pltpu.SemaphoreType.DMA((2,)),
        ],
    )
    return pl.pallas_call(
        functools.partial(_expert_kernel, nb=nb, n_slots=n_slots),
        grid_spec=grid_spec,
        out_shape=jax.ShapeDtypeStruct(((n_slots + 2 * blk) * TOK_ROWS, LANES), F32),
        compiler_params=_params(1),
        name="moe_experts",
    )(block_e, src3, src3, dst3, dst3, gate3, wg, wu, wd, xm)


def moe_tables(eid, gate, n_tok):
    blk = MOE_BLK
    n_assign = 2 * n_tok
    flat_e = eid[:2].reshape(-1)
    flat_g = gate[:2].reshape(-1)
    counts = jnp.sum((flat_e[:, None] == jnp.arange(N_EXP, dtype=jnp.int32)[None, :]).astype(jnp.int32), axis=0)
    padded = (counts + blk - 1) // blk * blk
    pad_end = jnp.cumsum(padded)
    pad_start = pad_end - padded
    cnt_start = jnp.cumsum(counts) - counts
    order = jnp.argsort(flat_e).astype(jnp.int32)
    nb = -(-(n_assign + N_EXP * (blk - 1)) // blk)
    bstart = jnp.arange(nb, dtype=jnp.int32) * blk
    block_e = jnp.minimum(jnp.sum((pad_end[None, :] <= bstart[:, None]).astype(jnp.int32), axis=1), N_EXP - 1)
    lane = jnp.arange(blk, dtype=jnp.int32)[None, :]
    j = (bstart - pad_start[block_e])[:, None] + lane
    cnt_b = counts[block_e][:, None]
    valid = j < cnt_b
    n_valid_before = cnt_start[block_e][:, None] + jnp.minimum(j, cnt_b)
    a = order[jnp.clip(n_valid_before, 0, n_assign - 1)]
    slot = bstart[:, None] + lane
    slot_src = jnp.where(valid, jnp.where(a >= n_tok, a - n_tok, a), 0).astype(jnp.int32)
    slot_dst = jnp.where(valid, a, n_assign + slot - n_valid_before).astype(jnp.int32)
    slot_gate = jnp.where(valid, flat_g[a], 0.0)
    return block_e.astype(jnp.int32), slot_src, slot_dst, slot_gate


def _combine_kernel(x_ref, y0_ref, y1_ref, m_ref, g_ref, b_ref, o_ref):
    tm = x_ref.shape[0]
    y = jnp.concatenate([y0_ref[pl.ds(j, tm, stride=TOK_ROWS), :] + y1_ref[pl.ds(j, tm, stride=TOK_ROWS), :]
                         for j in range(TOK_ROWS)], axis=1)
    o_ref[...] = _layer_norm(ALPHA * x_ref[...] + m_ref[5:6, :] * y, g_ref[...], b_ref[...])


def moe_combine(x1, Y, mvec, lng, lnb, *, n_tok, row_off, tm=512):
    B, T, _ = x1.shape
    tm = min(tm, T)
    nt = T // tm
    rb = row_off // tm
    kb = n_tok // tm
    return pl.pallas_call(
        _combine_kernel,
        grid=(B, nt),
        in_specs=[
            pl.BlockSpec((None, tm, D), lambda b, i: (b, i, 0)),
            pl.BlockSpec((tm * TOK_ROWS, LANES), lambda b, i: (rb + b * nt + i, 0)),
            pl.BlockSpec((tm * TOK_ROWS, LANES), lambda b, i: (kb + rb + b * nt + i, 0)),
            pl.BlockSpec((None, SUBLANES, D), lambda b, i: (b, 0, 0)),
            _const_spec((1, D)), _const_spec((1, D)),
        ],
        out_specs=pl.BlockSpec((None, tm, D), lambda b, i: (b, i, 0)),
        out_shape=jax.ShapeDtypeStruct((B, T, D), F32),
        compiler_params=_params(2),
        name="moe_combine",
    )(x1, Y, Y, mvec, lng.reshape(1, D), lnb.reshape(1, D))


def _even_w_in(w):
    g0 = GLA_QK * 2 + GLA_V * 2
    g1 = g0 + 2 * GLA_RANK
    pad = jnp.zeros((D, EV_NP - w.shape[1]), w.dtype)
    return jnp.concatenate([w[:, :g0], w[:, g1:], w[:, g0:g1], pad], axis=1).astype(BF16)


def _odd_w_in(w):
    return jnp.concatenate([w, jnp.zeros((D, OD_NP - w.shape[1]), w.dtype)], axis=1).astype(BF16)


def _lru_weights(w_r, b_r, w_i, b_i):
    wdir = jnp.concatenate([w_r, w_i], axis=-1).astype(BF16)
    blk = lambda b: b.reshape(2, LRU_NB, 1, LRU_BW)
    bdir = jnp.concatenate([blk(b_r), blk(b_i)], axis=-1)
    return wdir, bdir


def _gla_gate_weights(w_a2, b_a):
    wa = jnp.zeros((2, LANES, GLA_QK), F32)
    wa = wa.at[0, :GLA_RANK].set(w_a2[0]).at[1, GLA_RANK:2 * GLA_RANK].set(w_a2[1])
    w_hi, w_lo = _split2(wa)
    return jnp.concatenate([w_hi, w_lo, w_hi], axis=1), b_a.reshape(2, 1, GLA_QK)


def _mlstm_gate_table(P, n_col):
    B, T, _ = P.shape
    g = P[:, :, OD_GATE_COL:OD_GATE_COL + 4 * ML_H].reshape(B, T // n_col, n_col, 4, ML_H)
    g = jnp.transpose(g, (0, 4, 2, 3, 1))
    return jnp.concatenate([g, jnp.zeros_like(g)], axis=3)


def _mvec(mods_l, rows):
    m6 = jnp.stack([mods_l[r] for r in rows]).reshape(len(rows), 6, D)
    return jnp.concatenate([m6, jnp.zeros((len(rows), 2, D), F32)], axis=1)


def kernel(x, c, ctx, c_ctx, w_mod, b_mod, ln_g, ln_b, ev_w_in, gla_w_a2, gla_b_a, gla_norm, lru_conv_w, lru_conv_b, lru_w_r, lru_b_r, lru_w_i, lru_b_i, lru_lam, ev_w_out, od_w_in, mlstm_b_gate, mlstm_norm, od_w_out, moe_w_group, moe_b_group, moe_w_expert, moe_b_expert, moe_w_gate, moe_w_up, moe_w_down):
    B, S, _ = x.shape
    T_ctx = ctx.shape[1]
    cond8 = jnp.zeros((SUBLANES, D), F32).at[:B].set(c).at[B].set(c_ctx)
    mods = cond_mod(cond8, w_mod, b_mod)
    h_ctx = ctx
    for l in range(DEPTH):
        j = l // 2
        last = l == DEPTH - 1
        m_lat = _mvec(mods[l], list(range(B)))
        m_ctx = _mvec(mods[l], [B] * B)
        rw = _router_weights(moe_w_group[l], moe_b_group[l], moe_w_expert[l], moe_b_expert[l])
        n_tok = B * S if last else B * (S + T_ctx)
        if l % 2 == 0:
            w_in = _even_w_in(ev_w_in[j])
            wa, ba = _gla_gate_weights(gla_w_a2[j], gla_b_a[j])
            wdir, bdir = _lru_weights(lru_w_r[j], lru_b_r[j], lru_w_i[j], lru_b_i[j])
            w_out = ev_w_out[j].astype(BF16)

            def mix_even(xin, mv, s0, h0):
                P = inproj(xin, mv, w_in, tn=EV_TN)
                of, ob, sfin = gla_scan(P, wa, ba, s0)
                hf, hb, hfin = lru_mix(P, lru_conv_w[j], lru_conv_b[j], wdir, bdir, lru_lam[j], h0)
                return (of, ob, P, hf, hb), sfin, hfin

            s0 = jnp.zeros((B, 2, GLA_H, GLA_DV, GLA_DK), F32)
            h0 = jnp.zeros((B, 2, SUBLANES, LRU_W), F32)
            acts_c, s_c, h_c = mix_even(h_ctx, m_ctx, s0, h0)
            acts_l, _, _ = mix_even(x, m_lat, s_c, h_c)
            carry = None
            if not last:
                hc1, *carry = out_block("even", acts_c, h_ctx, m_ctx, gla_norm[j], w_out, ln_g[l, 0], ln_b[l, 0], rw,
                                        n_tok_all=n_tok, row_off=B * S)
            x1, xm, eid, gate = out_block("even", acts_l, x, m_lat, gla_norm[j], w_out, ln_g[l, 0], ln_b[l, 0], rw,
                                          n_tok_all=n_tok, row_off=0, carry=carry)
        else:
            w_in = _odd_w_in(od_w_in[j])
            w_out = od_w_out[j].astype(BF16)
            bgate = jnp.concatenate([mlstm_b_gate[j].T, jnp.zeros((ML_H, 4), F32)], axis=1).reshape(ML_H, SUBLANES, 1)

            def mix_odd(xin, mv, c0, m0, n_col):
                P = inproj(xin, mv, w_in, tn=OD_TN)
                gt = _mlstm_gate_table(P, n_col)
                halves, cfin, mfin = mlstm_scan(P, gt, bgate, c0, m0, n_col=n_col)
                return (halves, P), cfin, mfin

            c0 = jnp.zeros((B, 2, ML_H, ML_DK, ML_AUG), F32)
            m0 = jnp.full((B, 2, ML_H, SUBLANES, LANES), M_INIT, F32)
            acts_c, c_c, m_c = mix_odd(h_ctx, m_ctx, c0, m0, 1)
            acts_l, _, _ = mix_odd(x, m_lat, c_c, m_c, GRID_W)
            carry = None
            if not last:
                hc1, *carry = out_block("odd", acts_c, h_ctx, m_ctx, mlstm_norm[j], w_out, ln_g[l, 0], ln_b[l, 0], rw,
                                        n_tok_all=n_tok, row_off=B * S)
            x1, xm, eid, gate = out_block("odd", acts_l, x, m_lat, mlstm_norm[j], w_out, ln_g[l, 0], ln_b[l, 0], rw,
                                          n_tok_all=n_tok, row_off=0, carry=carry)
        block_e, slot_src, slot_dst, slot_gate = moe_tables(eid, gate, n_tok)
        Y = moe_experts(xm, block_e, slot_src, slot_dst, slot_gate, moe_w_gate, moe_w_up, moe_w_down, l)
        x = moe_combine(x1, Y, m_lat, ln_g[l, 1], ln_b[l, 1], n_tok=n_tok, row_off=0)
        if not last:
            h_ctx = moe_combine(hc1, Y, m_ctx, ln_g[l, 1], ln_b[l, 1], n_tok=n_tok, row_off=B * S)
    return x
```

```python
import functools

import jax
import jax.numpy as jnp
from jax import lax
from jax.experimental import pallas as pl
from jax.experimental.pallas import tpu as pltpu

F32 = jnp.float32
BF16 = jnp.bfloat16

D = 2048
DEPTH = 2
GRID_W = 64
CHUNK = 64
GLA_H, GLA_DK, GLA_DV, GLA_RANK, GLA_TAU = 4, 128, 256, 16, 16.0
GLA_QK, GLA_V = GLA_H * GLA_DK, GLA_H * GLA_DV
LRU_W, LRU_NB, LRU_C, CONV_W = 1024, 8, 8.0, 4
LRU_BW = LRU_W // LRU_NB
ML_H, ML_DK, ML_DV = 8, 128, 256
ML_QK, ML_V = ML_H * ML_DK, ML_H * ML_DV
M_INIT = -1e30
N_GROUPS, EPG, N_EXP, D_EXP = 4, 8, 32, 512
ALPHA = (2 * DEPTH) ** 0.25
EPS = 1e-5

LANES = 128
SUBLANES = 8
VMEM_LIMIT = 56 * 1024 * 1024

EV_NP = 5376
EV_TN = 1792
EV_GATE_COL = 5120
OD_NP = 6400
OD_TN = 1280
OD_GATE_COL = 6144
MOE_BLK = 256
TOK_ROWS = D // LANES
TOK_PITCH = 24


def _params(n_axes):
    return pltpu.CompilerParams(dimension_semantics=("arbitrary",) * n_axes, vmem_limit_bytes=VMEM_LIMIT)


def _sigmoid(x):
    return 1.0 / (1.0 + jnp.exp(-x))


def _sigmoid_tanh(x):
    return 0.5 * jnp.tanh(0.5 * x) + 0.5


def _silu(x):
    return x * _sigmoid(x)


def _log_sigmoid(x):
    return jnp.minimum(x, 0.0) - jnp.log1p(jnp.exp(-jnp.abs(x)))


def _gelu_tanh(x):
    return 0.5 * x * (1.0 + jnp.tanh(0.7978845608028654 * (x + 0.044715 * (x * x * x))))


def _bdot(a, b):
    return jnp.dot(a.astype(BF16), b.astype(BF16), preferred_element_type=F32)


def _bdot_nt(a, b):
    return lax.dot_general(a.astype(BF16), b.astype(BF16), (((1,), (1,)), ((), ())), preferred_element_type=F32)


def _bdot_tn(a, b):
    return lax.dot_general(a.astype(BF16), b.astype(BF16), (((0,), (0,)), ((), ())), preferred_element_type=F32)


def _split2(x):
    hi = x.astype(BF16)
    return hi, (x - hi.astype(F32)).astype(BF16)


def _split3(x):
    hi = x.astype(BF16)
    r = x - hi.astype(F32)
    mid = r.astype(BF16)
    return hi, mid, (r - mid.astype(F32)).astype(BF16)


def _mod_kernel(c_ref, w_ref, b_ref, o_ref):
    o_ref[...] = _bdot(_silu(c_ref[...]), w_ref[...]) + b_ref[...]


def cond_mod(cond8, w_mod, b_mod):
    nl, _, n6 = w_mod.shape
    tn = 1024
    return pl.pallas_call(
        _mod_kernel,
        grid=(nl, n6 // tn),
        in_specs=[
            pl.BlockSpec((SUBLANES, D), lambda l, j: (0, 0)),
            pl.BlockSpec((None, D, tn), lambda l, j: (l, 0, j)),
            pl.BlockSpec((None, 1, tn), lambda l, j: (l, 0, j)),
        ],
        out_specs=pl.BlockSpec((None, SUBLANES, tn), lambda l, j: (l, 0, j)),
        out_shape=jax.ShapeDtypeStruct((nl, SUBLANES, n6), F32),
        compiler_params=_params(2),
        name="cond_mod",
    )(cond8, w_mod, b_mod.reshape(nl, 1, n6))


def _inproj_kernel(x_ref, m_ref, w_ref, o_ref, u_ref):
    @pl.when(pl.program_id(2) == 0)
    def _():
        u_ref[...] = (x_ref[...] * (1.0 + m_ref[1:2, :]) + m_ref[0:1, :]).astype(BF16)

    o_ref[...] = jnp.dot(u_ref[...], w_ref[...], preferred_element_type=F32)


def inproj(x3, mvec, w, *, tn, tm=1024):
    B, T, _ = x3.shape
    Np = w.shape[1]
    tm = min(tm, T)
    return pl.pallas_call(
        _inproj_kernel,
        grid=(B, T // tm, Np // tn),
        in_specs=[
            pl.BlockSpec((None, tm, D), lambda b, i, j: (b, i, 0)),
            pl.BlockSpec((None, SUBLANES, D), lambda b, i, j: (b, 0, 0)),
            pl.BlockSpec((D, tn), lambda b, i, j: (0, j)),
        ],
        out_specs=pl.BlockSpec((None, tm, tn), lambda b, i, j: (b, i, j)),
        out_shape=jax.ShapeDtypeStruct((B, T, Np), F32),
        scratch_shapes=[pltpu.VMEM((tm, D), BF16)],
        compiler_params=_params(3),
        name="inproj",
    )(x3, mvec, w)


def _gla_kernel(qf, kf, vf, gf, qb, kb, vb, gb, wa_ref, ba_ref, s0_ref, of_ref, ob_ref, sfin_ref, st_ref, *, tb, ns):
    s = pl.program_id(1)

    @pl.when(s == 0)
    def _():
        st_ref[...] = s0_ref[...]

    L = CHUNK
    row = lax.broadcasted_iota(jnp.int32, (L, L), 0)
    col = lax.broadcasted_iota(jnp.int32, (L, L), 1)
    row3 = lax.broadcasted_iota(jnp.int32, (L, 3 * L), 0)
    col3 = lax.rem(lax.broadcasted_iota(jnp.int32, (L, 3 * L), 1), L)
    scale = GLA_DK ** -0.5
    for d, (q_ref, k_ref, v_ref, g_ref, o_ref) in enumerate(((qf, kf, vf, gf, of_ref), (qb, kb, vb, gb, ob_ref))):
        mask = (row >= col) if d == 0 else (row <= col)
        m3 = jnp.where((row3 >= col3) if d == 0 else (row3 <= col3), 1.0, 0.0).astype(BF16)
        x_hi, x_lo = _split2(g_ref[...])
        gpre = jnp.dot(jnp.concatenate([x_hi, x_hi, x_lo], axis=1), wa_ref[d], preferred_element_type=F32)
        glog = _log_sigmoid(gpre + ba_ref[d]) * (1.0 / GLA_TAU)
        nchunk = tb // L
        for c in (range(nchunk) if d == 0 else range(nchunk - 1, -1, -1)):
            r0 = c * L
            g1, g2, g3 = _split3(glog[r0:r0 + L, :])
            b_all = jnp.dot(m3, jnp.concatenate([g1, g2, g3], axis=0), preferred_element_type=F32)
            for h in range(GLA_H):
                q = q_ref[r0:r0 + L, h * GLA_DK:(h + 1) * GLA_DK]
                k = k_ref[r0:r0 + L, h * GLA_DK:(h + 1) * GLA_DK]
                v = v_ref[r0:r0 + L, h * GLA_DV:(h + 1) * GLA_DV].astype(BF16)
                b = b_all[:, h * GLA_DK:(h + 1) * GLA_DK]
                bl = b[L - 1:L, :] if d == 0 else b[0:1, :]
                qd = (q * (jnp.exp(b) * scale)).astype(BF16)
                ki = k * jnp.exp(-b)
                ke = k * jnp.exp(bl - b)
                att = jnp.where(mask, _bdot_nt(qd, ki), 0.0)
                st = st_ref[d, h]
                o = _bdot(att, v) + _bdot_nt(qd, st)
                o_ref[r0:r0 + L, h * GLA_DV:(h + 1) * GLA_DV] = o
                st_ref[d, h] = st * jnp.exp(bl) + _bdot_tn(v, ke)

    @pl.when(s == ns - 1)
    def _():
        sfin_ref[...] = st_ref[...]


def gla_scan(P, wa, ba, s0, *, tb=512):
    B, T, _ = P.shape
    tb = min(tb, T)
    ns = T // tb
    fwd = lambda b, s: s
    bwd = lambda b, s: ns - 1 - s

    def specs(pos):
        return [
            pl.BlockSpec((None, tb, GLA_QK), lambda b, s: (b, pos(b, s), 0)),
            pl.BlockSpec((None, tb, GLA_QK), lambda b, s: (b, pos(b, s), 1)),
            pl.BlockSpec((None, tb, GLA_V), lambda b, s: (b, pos(b, s), 1)),
            pl.BlockSpec((None, tb, LANES), lambda b, s: (b, pos(b, s), EV_GATE_COL // LANES)),
        ]

    st_shape = (2, GLA_H, GLA_DV, GLA_DK)
    st_spec = pl.BlockSpec((None,) + st_shape, lambda b, s: (b, 0, 0, 0, 0))
    return pl.pallas_call(
        functools.partial(_gla_kernel, tb=tb, ns=ns),
        grid=(B, ns),
        in_specs=specs(fwd) + specs(bwd) + [
            pl.BlockSpec((2, 3 * LANES, GLA_QK), lambda b, s: (0, 0, 0)),
            pl.BlockSpec((2, 1, GLA_QK), lambda b, s: (0, 0, 0)),
            st_spec,
        ],
        out_specs=[
            pl.BlockSpec((None, tb, GLA_V), lambda b, s: (b, s, 0)),
            pl.BlockSpec((None, tb, GLA_V), lambda b, s: (b, ns - 1 - s, 0)),
            st_spec,
        ],
        out_shape=[
            jax.ShapeDtypeStruct((B, T, GLA_V), F32),
            jax.ShapeDtypeStruct((B, T, GLA_V), F32),
            jax.ShapeDtypeStruct((B,) + st_shape, F32),
        ],
        scratch_shapes=[pltpu.VMEM(st_shape, F32)],
        compiler_params=_params(2),
        name="gla_scan",
    )(P, P, P, P, P, P, P, P, wa, ba, s0)


def _lru_kernel(xf, pf, nf, xb, pb, nb_, cw_ref, cb_ref, w_ref, bias_ref, lam_ref, h0_ref,
                hf_ref, hb_ref, hfin_ref, a_s, u_s, h_s, *, tm, nt):
    i = pl.program_id(1)

    @pl.when(i == 0)
    def _():
        h_s[...] = h0_ref[...]

    cw = cw_ref[...]
    off = SUBLANES - CONV_W // 2
    for d, (x_ref, p_ref, n_ref) in enumerate(((xf, pf, nf), (xb, pb, nb_))):
        ti = i if d == 0 else nt - 1 - i
        prev = jnp.where(ti > 0, p_ref[...], 0.0)
        nxt = jnp.where(ti < nt - 1, n_ref[...], 0.0)
        ext = jnp.concatenate([prev, x_ref[...], nxt], axis=0)
        xc = cb_ref[...] + (ext[off:off + tm] * cw[0:1] + ext[off + 1:off + 1 + tm] * cw[1:2]
                            + ext[off + 2:off + 2 + tm] * cw[2:3] + ext[off + 3:off + 3 + tm] * cw[3:4])
        for n in range(LRU_NB):
            cs = slice(n * LRU_BW, (n + 1) * LRU_BW)
            xn = xc[:, cs]
            z = _bdot(xn, w_ref[d, n]) + bias_ref[d, n]
            r = _sigmoid_tanh(z[:, :LRU_BW])
            ig = _sigmoid_tanh(z[:, LRU_BW:])
            a = jnp.exp((LRU_C * r) * _log_sigmoid(lam_ref[d:d + 1, cs]))
            a_s[d, :, cs] = a
            u_s[d, :, cs] = jnp.sqrt(1.0 - a * a) * (ig * xn)

    sub = lax.broadcasted_iota(jnp.int32, (SUBLANES, LRU_W), 0)

    def group(g, carry):
        hf, hb = carry
        bf = pl.multiple_of(g * SUBLANES, SUBLANES)
        bb = pl.multiple_of(tm - SUBLANES - g * SUBLANES, SUBLANES)
        a_f, u_f = a_s[0, pl.ds(bf, SUBLANES), :], u_s[0, pl.ds(bf, SUBLANES), :]
        a_b, u_b = a_s[1, pl.ds(bb, SUBLANES), :], u_s[1, pl.ds(bb, SUBLANES), :]
        for j in range(SUBLANES):
            hf = jnp.where(sub == j, a_f * pltpu.roll(hf, 1, axis=0) + u_f, hf)
            jb = SUBLANES - 1 - j
            hb = jnp.where(sub == jb, a_b * pltpu.roll(hb, SUBLANES - 1, axis=0) + u_b, hb)
        hf_ref[pl.ds(bf, SUBLANES), :] = hf
        hb_ref[pl.ds(bb, SUBLANES), :] = hb
        return hf, hb

    hf, hb = lax.fori_loop(0, tm // SUBLANES, group, (h_s[0], h_s[1]))
    h_s[0] = hf
    h_s[1] = hb

    @pl.when(i == nt - 1)
    def _():
        hfin_ref[...] = h_s[...]


def lru_mix(P, conv_w, conv_b, wdir, bdir, lam, h0, *, tm=512):
    B, T, _ = P.shape
    tm = min(tm, T)
    nt = T // tm
    xcol = 3
    r8 = tm // SUBLANES
    n8 = T // SUBLANES
    fwd = lambda i: i
    bwd = lambda i: nt - 1 - i

    def xspecs(pos):
        return [
            pl.BlockSpec((None, tm, LRU_W), lambda b, i: (b, pos(i), xcol)),
            pl.BlockSpec((None, SUBLANES, LRU_W), lambda b, i: (b, jnp.maximum(pos(i) * r8 - 1, 0), xcol)),
            pl.BlockSpec((None, SUBLANES, LRU_W), lambda b, i: (b, jnp.minimum((pos(i) + 1) * r8, n8 - 1), xcol)),
        ]

    hspec = pl.BlockSpec((None, 2, SUBLANES, LRU_W), lambda b, i: (b, 0, 0, 0))
    seq = jax.ShapeDtypeStruct((B, T, LRU_W), F32)
    return pl.pallas_call(
        functools.partial(_lru_kernel, tm=tm, nt=nt),
        grid=(B, nt),
        in_specs=xspecs(fwd) + xspecs(bwd) + [
            _const_spec((CONV_W, LRU_W)), _const_spec((1, LRU_W)),
            _const_spec((2, LRU_NB, LRU_BW, 2 * LRU_BW)), _const_spec((2, LRU_NB, 1, 2 * LRU_BW)),
            _const_spec((2, LRU_W)), hspec,
        ],
        out_specs=[
            pl.BlockSpec((None, tm, LRU_W), lambda b, i: (b, i, 0)),
            pl.BlockSpec((None, tm, LRU_W), lambda b, i: (b, nt - 1 - i, 0)),
            hspec,
        ],
        out_shape=[seq, seq, jax.ShapeDtypeStruct((B, 2, SUBLANES, LRU_W), F32)],
        scratch_shapes=[pltpu.VMEM((2, tm, LRU_W), F32), pltpu.VMEM((2, tm, LRU_W), F32),
                        pltpu.VMEM((2, SUBLANES, LRU_W), F32)],
        compiler_params=_params(2),
        name="lru_mix",
    )(P, P, P, P, P, P, conv_w, conv_b.reshape(1, LRU_W), wdir, bdir, lam, h0)


ML_AUG = ML_DV + LANES
ML_L = 256


def _chunk_rows(ref, w):
    L, nwl, width = ref.shape
    flat = ref.reshape(L * nwl, width)
    return flat.at[pl.ds(w, L, stride=nwl), :] if nwl > 1 else flat


def _mlstm_kernel(qf, kf, vf0, vf1, gf, qb, kb, vb0, vb1, gb, bg_ref, c0_ref, m0_ref,
                  hf0_ref, hf1_ref, hb0_ref, hb1_ref, cfin_ref, mfin_ref, c_ref, m_ref, *, L, nwl, ns):
    s = pl.program_id(2)

    @pl.when(s == 0)
    def _():
        c_ref[...] = c0_ref[...]
        m_ref[...] = m0_ref[...]

    row = lax.broadcasted_iota(jnp.int32, (L, L), 0)
    col = lax.broadcasted_iota(jnp.int32, (L, L), 1)
    nr = nwl * SUBLANES
    sub = lax.rem(lax.broadcasted_iota(jnp.int32, (nr, L), 0), SUBLANES)
    lane = lax.broadcasted_iota(jnp.int32, (nr, L), 1)
    is_f = (sub == 1) | (sub == 3)
    ones_col = (lax.broadcasted_iota(jnp.int32, (L, LANES), 1) == 0).astype(BF16)
    kscale = ML_DK ** -0.5
    bias = jnp.concatenate([bg_ref[...]] * nwl, axis=0)

    chunk_rows = _chunk_rows

    def gate_vectors(g_ref, d):
        pre = g_ref[...].reshape(nr, L) + bias
        xg = jnp.where(is_f, _log_sigmoid(pre), pre)
        cs = xg
        k_ = 1
        while k_ < L:
            if d == 0:
                cs = cs + jnp.where(lane >= k_, pltpu.roll(cs, k_, axis=1), 0.0)
            else:
                cs = cs + jnp.where(lane < L - k_, pltpu.roll(cs, L - k_, axis=1), 0.0)
            k_ *= 2
        ri, rf = 2 * d, 2 * d + 1
        pad = jnp.zeros((LANES - nr, L), F32)
        cs_t = jnp.concatenate([cs, pad], axis=0).T
        xg_t = jnp.concatenate([xg, pad], axis=0).T
        out = []
        for w in range(nwl):
            jb, ji = SUBLANES * w + rf, SUBLANES * w + ri
            out.append((cs[jb:jb + 1, :], xg[ji:ji + 1, :], cs_t[:, jb:jb + 1], xg_t[:, ji:ji + 1]))
        return out

    vec = (gate_vectors(gf, 0), gate_vectors(gb, 1))
    state = [(m_ref[d][0:1, 0:1], c_ref[d]) for d in range(2)]
    work = [(0, w, qf, kf, (vf0, vf1), (hf0_ref, hf1_ref)) for w in range(nwl)]
    work_b = [(1, w, qb, kb, (vb0, vb1), (hb0_ref, hb1_ref)) for w in range(nwl - 1, -1, -1)]
    for pair in zip(work, work_b):
      for d, w, q_ref, k_ref, v_refs, h_refs in pair:
        mask = (row >= col) if d == 0 else (row <= col)
        b_row, i_row, b_col, i_col = vec[d][w]
        b_last = b_row[:, L - 1:L] if d == 0 else b_row[:, 0:1]
        dmat = jnp.where(mask, b_col + (i_row - b_row), -jnp.inf)
        m_loc = jnp.max(dmat, axis=1, keepdims=True)
        q = chunk_rows(q_ref, w)[...].astype(BF16)
        k = chunk_rows(k_ref, w)[...] * kscale
        vaug = jnp.concatenate([chunk_rows(v_refs[0], w)[...].astype(BF16), chunk_rows(v_refs[1], w)[...].astype(BF16),
                                ones_col], axis=1)
        sc = _bdot_nt(q, k) * jnp.exp(dmat - m_loc)
        n_loc = _bdot(sc, vaug)
        g_row = b_last - b_row + i_row
        m_k = jnp.max(g_row, axis=1, keepdims=True)
        kv_loc = _bdot_tn(jnp.exp(b_last - b_col + i_col - m_k) * k, vaug)
        m, caug = state[d]
        inter = b_col + m
        m_t = jnp.maximum(inter, m_loc)
        acc = jnp.exp(m_loc - m_t) * n_loc + jnp.exp(inter - m_t) * _bdot(q, caug)
        den = acc[:, ML_DV:ML_DV + 1]
        h = acc[:, :ML_DV] / jnp.maximum(jnp.abs(den), jnp.exp(-m_t))
        chunk_rows(h_refs[0], w)[...] = h[:, :LANES]
        chunk_rows(h_refs[1], w)[...] = h[:, LANES:]
        m_new = jnp.maximum(b_last + m, m_k)
        state[d] = (m_new, jnp.exp(b_last + m - m_new) * caug + jnp.exp(m_k - m_new) * kv_loc)

    for d in range(2):
        m_ref[d] = jnp.broadcast_to(state[d][0], (SUBLANES, LANES))
        c_ref[d] = state[d][1]

    @pl.when(s == ns - 1)
    def _():
        cfin_ref[...] = c_ref[...]
        mfin_ref[...] = m_ref[...]


def mlstm_scan(P, gt, bgate, c0, m0, *, n_col):
    B, T, Np = P.shape
    L = T // n_col
    nwl = min(n_col, SUBLANES)
    ns = n_col // nwl
    P5 = P.reshape(B, L, ns, nwl, Np)
    fwd = lambda s: s
    bwd = lambda s: ns - 1 - s

    def specs(pos):
        return [
            pl.BlockSpec((None, L, None, nwl, ML_DK), lambda b, h, s: (b, 0, pos(s), 0, h)),
            pl.BlockSpec((None, L, None, nwl, ML_DK), lambda b, h, s: (b, 0, pos(s), 0, ML_H + h)),
            pl.BlockSpec((None, L, None, nwl, LANES), lambda b, h, s: (b, 0, pos(s), 0, 2 * ML_H + 2 * h)),
            pl.BlockSpec((None, L, None, nwl, LANES), lambda b, h, s: (b, 0, pos(s), 0, 2 * ML_H + 2 * h + 1)),
            pl.BlockSpec((None, None, nwl, SUBLANES, L), lambda b, h, s: (b, h, pos(s), 0, 0)),
        ]

    cshape = (2, ML_DK, ML_AUG)
    mshape = (2, SUBLANES, LANES)
    cspec = pl.BlockSpec((None, 2, None, ML_DK, ML_AUG), lambda b, h, s: (b, 0, h, 0, 0))
    mspec = pl.BlockSpec((None, 2, None, SUBLANES, LANES), lambda b, h, s: (b, 0, h, 0, 0))
    seq = jax.ShapeDtypeStruct((B, L, ns, nwl, ML_H * LANES), F32)
    ospec = lambda pos: pl.BlockSpec((None, L, None, nwl, LANES), lambda b, h, s: (b, 0, pos(s), 0, h))
    hf0, hf1, hb0, hb1, cfin, mfin = pl.pallas_call(
        functools.partial(_mlstm_kernel, L=L, nwl=nwl, ns=ns),
        grid=(B, ML_H, ns),
        in_specs=specs(fwd) + specs(bwd) + [
            pl.BlockSpec((None, SUBLANES, 1), lambda b, h, s: (h, 0, 0)),
            cspec, mspec,
        ],
        out_specs=[ospec(fwd), ospec(fwd), ospec(bwd), ospec(bwd), cspec, mspec],
        out_shape=[
            seq, seq, seq, seq,
            jax.ShapeDtypeStruct((B, 2, ML_H, ML_DK, ML_AUG), F32),
            jax.ShapeDtypeStruct((B, 2, ML_H, SUBLANES, LANES), F32),
        ],
        scratch_shapes=[pltpu.VMEM(cshape, F32), pltpu.VMEM(mshape, F32)],
        compiler_params=_params(3),
        name="mlstm_scan",
    )(P5, P5, P5, P5, gt, P5, P5, P5, P5, gt, bgate, c0, m0)
    flat = lambda a: a.reshape(B, T, ML_H * LANES)
    return (flat(hf0), flat(hf1), flat(hb0), flat(hb1)), cfin, mfin


def _layer_norm(z, g, b):
    mu = jnp.mean(z, axis=-1, keepdims=True)
    zc = z - mu
    var = jnp.mean(zc * zc, axis=-1, keepdims=True)
    return zc * lax.rsqrt(var + EPS) * g + b


def _head_rms(o, g, n_heads, width):
    parts = []
    for h in range(n_heads):
        oh = o[:, h * width:(h + 1) * width]
        parts.append(oh * lax.rsqrt(jnp.mean(oh * oh, axis=-1, keepdims=True) + EPS) * g)
    return jnp.concatenate(parts, axis=-1)


def _post(y, x_ref, m_ref, lng_ref, lnb_ref, rw_refs, x1_ref, xm_ref, eid_ref, gate_ref):
    w_hi, w_lo, bias = rw_refs
    x1 = _layer_norm(ALPHA * x_ref[...] + m_ref[2:3, :] * y, lng_ref[...], lnb_ref[...])
    x1_ref[...] = x1
    xm = x1 * (1.0 + m_ref[4:5, :]) + m_ref[3:4, :]
    tm = xm.shape[0]
    for j in range(TOK_ROWS):
        xm_ref[pl.ds(j, tm, stride=TOK_ROWS), :] = xm[:, j * LANES:(j + 1) * LANES]
    hi, lo = _split2(xm)
    lg = (jnp.dot(hi, w_hi[...], preferred_element_type=F32) + jnp.dot(hi, w_lo[...], preferred_element_type=F32)
          + jnp.dot(lo, w_hi[...], preferred_element_type=F32)) + bias[...]
    lane = lax.broadcasted_iota(jnp.int32, lg.shape, 1)
    neg = -jnp.inf
    glm = jnp.where(lane < N_GROUPS, lg, neg)
    gmax = jnp.max(glm, axis=-1, keepdims=True)
    gidx = jnp.min(jnp.where(glm == gmax, lane, LANES), axis=-1, keepdims=True)
    p_g = 1.0 / jnp.sum(jnp.exp(glm - gmax), axis=-1, keepdims=True)
    e_lane = lane - N_GROUPS
    in_group = (lax.shift_right_arithmetic(e_lane, 3) == gidx) & (e_lane < N_EXP)
    elm = jnp.where(in_group, lg, neg)
    m1 = jnp.max(elm, axis=-1, keepdims=True)
    i1 = jnp.min(jnp.where(elm == m1, e_lane, LANES), axis=-1, keepdims=True)
    elm2 = jnp.where(e_lane == i1, neg, elm)
    m2 = jnp.max(elm2, axis=-1, keepdims=True)
    i2 = jnp.min(jnp.where(elm2 == m2, e_lane, LANES), axis=-1, keepdims=True)
    e2 = jnp.exp(m2 - m1)
    g1 = p_g / (1.0 + e2)
    g2 = g1 * e2
    eid_ref[...] = jnp.where(lane == 0, i1, jnp.where(lane == 1, i2, 0)).T[:SUBLANES, :]
    gate_ref[...] = jnp.where(lane == 0, g1, jnp.where(lane == 1, g2, 0.0)).T[:SUBLANES, :]


def _out_even_kernel(of, ob, r, hf, hb, xg, x, m, gg, wout, lng, lnb, w_hi, w_lo, bias, *rest):
    x1_ref, xm_ref, eid_ref, gate_ref = rest[-4:]
    rg = r[...]
    o = _head_rms(of[...] + ob[...], gg[...], GLA_H, GLA_DV) * (rg * _sigmoid_tanh(rg))
    y2 = (hf[...] + hb[...]) * _gelu_tanh(xg[...])
    mix = jnp.concatenate([o.astype(BF16), y2.astype(BF16)], axis=-1)
    y = jnp.dot(mix, wout[...], preferred_element_type=F32)
    _post(y, x, m, lng, lnb, (w_hi, w_lo, bias), x1_ref, xm_ref, eid_ref, gate_ref)


def _out_odd_kernel(hf0, hf1, hb0, hb1, og, x, m, gg, wout, lng, lnb, w_hi, w_lo, bias, *rest):
    x1_ref, xm_ref, eid_ref, gate_ref = rest[-4:]
    lo = hf0[...] + hb0[...]
    hi = hf1[...] + hb1[...]
    hsum = jnp.concatenate([half[:, h * LANES:(h + 1) * LANES] for h in range(ML_H) for half in (lo, hi)], axis=-1)
    o = _head_rms(hsum, gg[...], ML_H, ML_DV) * _sigmoid_tanh(og[...])
    y = jnp.dot(o.astype(BF16), wout[...], preferred_element_type=F32)
    _post(y, x, m, lng, lnb, (w_hi, w_lo, bias), x1_ref, xm_ref, eid_ref, gate_ref)


def _router_weights(w_group, b_group, w_expert, b_expert):
    n = N_GROUPS + N_EXP
    wp = jnp.zeros((D, LANES), F32).at[:, :N_GROUPS].set(w_group).at[:, N_GROUPS:n].set(w_expert)
    bias = jnp.zeros((1, LANES), F32).at[0, :N_GROUPS].set(b_group).at[0, N_GROUPS:n].set(b_expert)
    w_hi, w_lo = _split2(wp)
    return w_hi, w_lo, bias


def _const_spec(shape):
    nd = len(shape)
    return pl.BlockSpec(shape, lambda *_: (0,) * nd)


def out_block(kind, acts, x, mvec, gg, wout, lng, lnb, rw, *, n_tok_all, row_off, carry=None, tm=256):
    B, T, _ = x.shape
    tm = min(tm, T)
    nt = T // tm
    tok = lambda width: pl.BlockSpec((None, tm, width), lambda b, i: (b, i, 0))
    act = lambda width, cb=0: pl.BlockSpec((None, tm, width), lambda b, i: (b, i, cb))
    if kind == "even":
        of, ob, P, hf, hb = acts
        a_in = [of, ob, P, hf, hb, P]
        a_specs = [act(GLA_V), act(GLA_V), act(GLA_V, 2), act(LRU_W), act(LRU_W), act(LRU_W, 4)]
        body = _out_even_kernel
    else:
        halves, P = acts
        a_in = list(halves) + [P]
        a_specs = [act(ML_H * LANES)] * 4 + [act(ML_V, 2)]
        body = _out_odd_kernel
    w_in = [mvec, gg.reshape(1, -1), wout, lng.reshape(1, D), lnb.reshape(1, D)] + list(rw)
    w_specs = [pl.BlockSpec((None, SUBLANES, D), lambda b, i: (b, 0, 0))] + [_const_spec(a.shape) for a in w_in[1:]]
    rb = row_off // tm
    blk_idx = lambda b, i: rb + b * nt + i
    route = pl.BlockSpec((SUBLANES, tm), lambda b, i: (0, blk_idx(b, i)))
    out_specs = [tok(D), pl.BlockSpec((tm * TOK_ROWS, LANES), lambda b, i: (blk_idx(b, i), 0)), route, route]
    out_shape = [jax.ShapeDtypeStruct(x.shape, F32), jax.ShapeDtypeStruct((n_tok_all * TOK_ROWS, LANES), F32),
                 jax.ShapeDtypeStruct((SUBLANES, n_tok_all), jnp.int32), jax.ShapeDtypeStruct((SUBLANES, n_tok_all), F32)]
    c_in, c_specs, aliases = [], [], {}
    if carry is None and n_tok_all != B * T:
        carry = tuple(jnp.zeros(s.shape, s.dtype) for s in out_shape[1:])
    if carry is not None:
        c_in = list(carry)
        c_specs = [pl.BlockSpec(memory_space=pl.ANY)] * 3
        base = len(a_in) + 1 + len(w_in)
        aliases = {base: 1, base + 1: 2, base + 2: 3}
    x1, xm, eid, gate = pl.pallas_call(
        body,
        grid=(B, nt),
        in_specs=a_specs + [tok(D)] + w_specs + c_specs,
        out_specs=out_specs,
        out_shape=out_shape,
        input_output_aliases=aliases,
        compiler_params=_params(2),
        name="out_" + kind,
    )(*a_in, x, *w_in, *c_in)
    return x1, xm, eid, gate


def _expert_kernel(be_ref, src_c, src_n, dst_p, dst_c, g_ref, wg, wu, wd, xm_hbm, y_hbm, xbuf, ybuf, gs, ss,
                   *, nb, n_slots):
    del be_ref
    n = pl.program_id(0)
    blk, tr, tp = MOE_BLK, TOK_ROWS, TOK_PITCH

    def gather(idx_ref, slot):
        for r in range(blk):
            row = pl.multiple_of(idx_ref[0, r], tr)
            pltpu.make_async_copy(xm_hbm.at[pl.ds(row, tr)], xbuf.at[slot, pl.ds(r * tp, tr)], gs.at[slot]).start()

    def scatter(idx_ref, slot):
        for r in range(blk):
            row = pl.multiple_of(idx_ref[0, r], tr)
            pltpu.make_async_copy(ybuf.at[slot, pl.ds(r * tp, tr)], y_hbm.at[pl.ds(row, tr)], ss.at[slot]).start()

    def wait_gather(slot):
        pltpu.make_async_copy(xm_hbm.at[pl.ds(0, blk * tr)], xbuf.at[slot, pl.ds(0, blk * tr)], gs.at[slot]).wait()

    def wait_scatter(slot):
        pltpu.make_async_copy(ybuf.at[slot, pl.ds(0, blk * tr)], y_hbm.at[pl.ds(0, blk * tr)], ss.at[slot]).wait()

    def mlp(slot):
        xs, ys = xbuf.at[slot], ybuf.at[slot]
        xb = jnp.concatenate([xs[pl.ds(j, blk, stride=tp), :] for j in range(tr)], axis=1).astype(BF16)
        hg = jnp.dot(xb, wg[...].astype(BF16), preferred_element_type=F32)
        hu = jnp.dot(xb, wu[...].astype(BF16), preferred_element_type=F32)
        h = (hg * _sigmoid_tanh(hg) * hu).astype(BF16)
        y = jnp.dot(h, wd[...].astype(BF16), preferred_element_type=F32)
        g_col = jnp.broadcast_to(g_ref[...], (LANES, blk)).T
        for j in range(tr):
            ys[pl.ds(j, blk, stride=tp), :] = y[:, j * LANES:(j + 1) * LANES] * g_col

    s = lax.rem(n, 2)

    @pl.when(n == 0)
    def _():
        ybuf[...] = jnp.zeros_like(ybuf)
        pltpu.make_async_copy(ybuf.at[0, pl.ds(0, blk * tr)], y_hbm.at[pl.ds((n_slots + blk) * tr, blk * tr)],
                              ss.at[0]).start()
        gather(src_c, 0)

    wait_gather(s)
    wait_scatter(s)
    gather(src_n, 1 - s)
    scatter(dst_p, 1 - s)
    mlp(s)

    @pl.when(n == nb - 1)
    def _():
        scatter(dst_c, s)
        wait_gather(1 - s)
        wait_scatter(1 - s)
        wait_scatter(s)


def moe_experts(xm, block_e, slot_src, slot_dst, slot_gate, wg, wu, wd, layer):
    nb, blk = slot_src.shape
    n_slots = nb * blk
    src3 = (slot_src * TOK_ROWS).reshape(nb, 1, blk)
    spare = (n_slots + jnp.arange(blk, dtype=jnp.int32)).reshape(1, 1, blk)
    dst3 = jnp.concatenate([spare, slot_dst.reshape(nb, 1, blk)], axis=0) * TOK_ROWS
    gate3 = slot_gate.reshape(nb, 1, blk)
    smem_blk = lambda f: pl.BlockSpec((None, 1, blk), f, memory_space=pltpu.SMEM)
    grid_spec = pltpu.PrefetchScalarGridSpec(
        num_scalar_prefetch=1,
        grid=(nb,),
        in_specs=[
            smem_blk(lambda n, be: (n, 0, 0)),
            smem_blk(lambda n, be: (jnp.minimum(n + 1, nb - 1), 0, 0)),
            smem_blk(lambda n, be: (n, 0, 0)),
            smem_blk(lambda n, be: (n + 1, 0, 0)),
            pl.BlockSpec((None, 1, blk), lambda n, be: (n, 0, 0)),
            pl.BlockSpec((None, None, D, D_EXP), lambda n, be: (layer, be[n], 0, 0)),
            pl.BlockSpec((None, None, D, D_EXP), lambda n, be: (layer, be[n], 0, 0)),
            pl.BlockSpec((None, None, D_EXP, D), lambda n, be: (layer, be[n], 0, 0)),
            pl.BlockSpec(memory_space=pl.ANY),
        ],
        out_specs=pl.BlockSpec(memory_space=pl.ANY),
        scratch_shapes=[
            pltpu.VMEM((2, blk * TOK_PITCH, LANES), F32),
            pltpu.VMEM((2, blk * TOK_PITCH, LANES), F32),
            pltpu.SemaphoreType.DMA((2,)),
            pltpu.SemaphoreType.DMA((2,)),
        ],
    )
    return pl.pallas_call(
        functools.partial(_expert_kernel, nb=nb, n_slots=n_slots),
        grid_spec=grid_spec,
        out_shape=jax.ShapeDtypeStruct(((n_slots + 2 * blk) * TOK_ROWS, LANES), F32),
        compiler_params=_params(1),
        name="moe_experts",
    )(block_e, src3, src3, dst3, dst3, gate3, wg, wu, wd, xm)


def moe_tables(eid, gate, n_tok):
    blk = MOE_BLK
    n_assign = 2 * n_tok
    flat_e = eid[:2].reshape(-1)
    flat_g = gate[:2].reshape(-1)
    counts = jnp.sum((flat_e[:, None] == jnp.arange(N_EXP, dtype=jnp.int32)[None, :]).astype(jnp.int32), axis=0)
    padded = (counts + blk - 1) // blk * blk
    pad_end = jnp.cumsum(padded)
    pad_start = pad_end - padded
    cnt_start = jnp.cumsum(counts) - counts
    order = jnp.argsort(flat_e).astype(jnp.int32)
    nb = -(-(n_assign + N_EXP * (blk - 1)) // blk)
    bstart = jnp.arange(nb, dtype=jnp.int32) * blk
    block_e = jnp.minimum(jnp.sum((pad_end[None, :] <= bstart[:, None]).astype(jnp.int32), axis=1), N_EXP - 1)
    lane = jnp.arange(blk, dtype=jnp.int32)[None, :]
    j = (bstart - pad_start[block_e])[:, None] + lane
    cnt_b = counts[block_e][:, None]
    valid = j < cnt_b
    n_valid_before = cnt_start[block_e][:, None] + jnp.minimum(j, cnt_b)
    a = order[jnp.clip(n_valid_before, 0, n_assign - 1)]
    slot = bstart[:, None] + lane
    slot_src = jnp.where(valid, jnp.where(a >= n_tok, a - n_tok, a), 0).astype(jnp.int32)
    slot_dst = jnp.where(valid, a, n_assign + slot - n_valid_before).astype(jnp.int32)
    slot_gate = jnp.where(valid, flat_g[a], 0.0)
    return block_e.astype(jnp.int32), slot_src, slot_dst, slot_gate


def _combine_kernel(x_ref, y0_ref, y1_ref, m_ref, g_ref, b_ref, o_ref):
    tm = x_ref.shape[0]
    y = jnp.concatenate([y0_ref[pl.ds(j, tm, stride=TOK_ROWS), :] + y1_ref[pl.ds(j, tm, stride=TOK_ROWS), :]
                         for j in range(TOK_ROWS)], axis=1)
    o_ref[...] = _layer_norm(ALPHA * x_ref[...] + m_ref[5:6, :] * y, g_ref[...], b_ref[...])


def moe_combine(x1, Y, mvec, lng, lnb, *, n_tok, row_off, tm=512):
    B, T, _ = x1.shape
    tm = min(tm, T)
    nt = T // tm
    rb = row_off // tm
    kb = n_tok // tm
    return pl.pallas_call(
        _combine_kernel,
        grid=(B, nt),
        in_specs=[
            pl.BlockSpec((None, tm, D), lambda b, i: (b, i, 0)),
            pl.BlockSpec((tm * TOK_ROWS, LANES), lambda b, i: (rb + b * nt + i, 0)),
            pl.BlockSpec((tm * TOK_ROWS, LANES), lambda b, i: (kb + rb + b * nt + i, 0)),
            pl.BlockSpec((None, SUBLANES, D), lambda b, i: (b, 0, 0)),
            _const_spec((1, D)), _const_spec((1, D)),
        ],
        out_specs=pl.BlockSpec((None, tm, D), lambda b, i: (b, i, 0)),
        out_shape=jax.ShapeDtypeStruct((B, T, D), F32),
        compiler_params=_params(2),
        name="moe_combine",
    )(x1, Y, Y, mvec, lng.reshape(1, D), lnb.reshape(1, D))


def _even_w_in(w):
    g0 = GLA_QK * 2 + GLA_V * 2
    g1 = g0 + 2 * GLA_RANK
    pad = jnp.zeros((D, EV_NP - w.shape[1]), w.dtype)
    return jnp.concatenate([w[:, :g0], w[:, g1:], w[:, g0:g1], pad], axis=1).astype(BF16)


def _odd_w_in(w):
    return jnp.concatenate([w, jnp.zeros((D, OD_NP - w.shape[1]), w.dtype)], axis=1).astype(BF16)


def _lru_weights(w_r, b_r, w_i, b_i):
    wdir = jnp.concatenate([w_r, w_i], axis=-1).astype(BF16)
    blk = lambda b: b.reshape(2, LRU_NB, 1, LRU_BW)
    bdir = jnp.concatenate([blk(b_r), blk(b_i)], axis=-1)
    return wdir, bdir


def _gla_gate_weights(w_a2, b_a):
    wa = jnp.zeros((2, LANES, GLA_QK), F32)
    wa = wa.at[0, :GLA_RANK].set(w_a2[0]).at[1, GLA_RANK:2 * GLA_RANK].set(w_a2[1])
    w_hi, w_lo = _split2(wa)
    return jnp.concatenate([w_hi, w_lo, w_hi], axis=1), b_a.reshape(2, 1, GLA_QK)


def _mlstm_gate_table(P, n_col):
    B, T, _ = P.shape
    g = P[:, :, OD_GATE_COL:OD_GATE_COL + 4 * ML_H].reshape(B, T // n_col, n_col, 4, ML_H)
    g = jnp.transpose(g, (0, 4, 2, 3, 1))
    return jnp.concatenate([g, jnp.zeros_like(g)], axis=3)


def _mvec(mods_l, rows):
    m6 = jnp.stack([mods_l[r] for r in rows]).reshape(len(rows), 6, D)
    return jnp.concatenate([m6, jnp.zeros((len(rows), 2, D), F32)], axis=1)


def kernel(x, c, ctx, c_ctx, w_mod, b_mod, ln_g, ln_b, ev_w_in, gla_w_a2, gla_b_a, gla_norm, lru_conv_w, lru_conv_b, lru_w_r, lru_b_r, lru_w_i, lru_b_i, lru_lam, ev_w_out, od_w_in, mlstm_b_gate, mlstm_norm, od_w_out, moe_w_group, moe_b_group, moe_w_expert, moe_b_expert, moe_w_gate, moe_w_up, moe_w_down):
    B, S, _ = x.shape
    T_ctx = ctx.shape[1]
    cond8 = jnp.zeros((SUBLANES, D), F32).at[:B].set(c).at[B].set(c_ctx)
    mods = cond_mod(cond8, w_mod, b_mod)
    h_ctx = ctx
    for l in range(DEPTH):
        j = l // 2
        last = l == DEPTH - 1
        m_lat = _mvec(mods[l], list(range(B)))
        m_ctx = _mvec(mods[l], [B] * B)
        rw = _router_weights(moe_w_group[l], moe_b_group[l], moe_w_expert[l], moe_b_expert[l])
        n_tok = B * S if last else B * (S + T_ctx)
        if l % 2 == 0:
            w_in = _even_w_in(ev_w_in[j])
            wa, ba = _gla_gate_weights(gla_w_a2[j], gla_b_a[j])
            wdir, bdir = _lru_weights(lru_w_r[j], lru_b_r[j], lru_w_i[j], lru_b_i[j])
            w_out = ev_w_out[j].astype(BF16)

            def mix_even(xin, mv, s0, h0):
                P = inproj(xin, mv, w_in, tn=EV_TN)
                of, ob, sfin = gla_scan(P, wa, ba, s0)
                hf, hb, hfin = lru_mix(P, lru_conv_w[j], lru_conv_b[j], wdir, bdir, lru_lam[j], h0)
                return (of, ob, P, hf, hb), sfin, hfin

            s0 = jnp.zeros((B, 2, GLA_H, GLA_DV, GLA_DK), F32)
            h0 = jnp.zeros((B, 2, SUBLANES, LRU_W), F32)
            acts_c, s_c, h_c = mix_even(h_ctx, m_ctx, s0, h0)
            acts_l, _, _ = mix_even(x, m_lat, s_c, h_c)
            carry = None
            if not last:
                hc1, *carry = out_block("even", acts_c, h_ctx, m_ctx, gla_norm[j], w_out, ln_g[l, 0], ln_b[l, 0], rw,
                                        n_tok_all=n_tok, row_off=B * S)
            x1, xm, eid, gate = out_block("even", acts_l, x, m_lat, gla_norm[j], w_out, ln_g[l, 0], ln_b[l, 0], rw,
                                          n_tok_all=n_tok, row_off=0, carry=carry)
        else:
            w_in = _odd_w_in(od_w_in[j])
            w_out = od_w_out[j].astype(BF16)
            bgate = jnp.concatenate([mlstm_b_gate[j].T, jnp.zeros((ML_H, 4), F32)], axis=1).reshape(ML_H, SUBLANES, 1)

            def mix_odd(xin, mv, c0, m0, n_col):
                P = inproj(xin, mv, w_in, tn=OD_TN)
                gt = _mlstm_gate_table(P, n_col)
                halves, cfin, mfin = mlstm_scan(P, gt, bgate, c0, m0, n_col=n_col)
                return (halves, P), cfin, mfin

            c0 = jnp.zeros((B, 2, ML_H, ML_DK, ML_AUG), F32)
            m0 = jnp.full((B, 2, ML_H, SUBLANES, LANES), M_INIT, F32)
            acts_c, c_c, m_c = mix_odd(h_ctx, m_ctx, c0, m0, 1)
            acts_l, _, _ = mix_odd(x, m_lat, c_c, m_c, GRID_W)
            carry = None
            if not last:
                hc1, *carry = out_block("odd", acts_c, h_ctx, m_ctx, mlstm_norm[j], w_out, ln_g[l, 0], ln_b[l, 0], rw,
                                        n_tok_all=n_tok, row_off=B * S)
            x1, xm, eid, gate = out_block("odd", acts_l, x, m_lat, mlstm_norm[j], w_out, ln_g[l, 0], ln_b[l, 0], rw,
                                          n_tok_all=n_tok, row_off=0, carry=carry)
        block_e, slot_src, slot_dst, slot_gate = moe_tables(eid, gate, n_tok)
        Y = moe_experts(xm, block_e, slot_src, slot_dst, slot_gate, moe_w_gate, moe_w_up, moe_w_down, l)
        x = moe_combine(x1, Y, m_lat, ln_g[l, 1], ln_b[l, 1], n_tok=n_tok, row_off=0)
        if not last:
            h_ctx = moe_combine(hc1, Y, m_ctx, ln_g[l, 1], ln_b[l, 1], n_tok=n_tok, row_off=B * S)
    return x
```

```python
import functools

import jax
import jax.numpy as jnp
from jax import lax
from jax.experimental import pallas as pl
from jax.experimental.pallas import tpu as pltpu

F32 = jnp.float32
BF16 = jnp.bfloat16

D = 2048
DEPTH = 2
GRID_W = 64
CHUNK = 64
GLA_H, GLA_DK, GLA_DV, GLA_RANK, GLA_TAU = 4, 128, 256, 16, 16.0
GLA_QK, GLA_V = GLA_H * GLA_DK, GLA_H * GLA_DV
LRU_W, LRU_NB, LRU_C, CONV_W = 1024, 8, 8.0, 4
LRU_BW = LRU_W // LRU_NB
ML_H, ML_DK, ML_DV = 8, 128, 256
ML_QK, ML_V = ML_H * ML_DK, ML_H * ML_DV
M_INIT = -1e30
N_GROUPS, EPG, N_EXP, D_EXP = 4, 8, 32, 512
ALPHA = (2 * DEPTH) ** 0.25
EPS = 1e-5

LANES = 128
SUBLANES = 8
VMEM_LIMIT = 56 * 1024 * 1024

EV_NP = 5376
EV_TN = 1792
EV_GATE_COL = 5120
OD_NP = 6400
OD_TN = 1280
OD_GATE_COL = 6144
MOE_BLK = 256
TOK_ROWS = D // LANES
TOK_PITCH = 24


def _params(n_axes):
    return pltpu.CompilerParams(dimension_semantics=("arbitrary",) * n_axes, vmem_limit_bytes=VMEM_LIMIT)


def _sigmoid(x):
    return 1.0 / (1.0 + jnp.exp(-x))


def _sigmoid_tanh(x):
    return 0.5 * jnp.tanh(0.5 * x) + 0.5


def _silu(x):
    return x * _sigmoid(x)


def _log_sigmoid(x):
    return jnp.minimum(x, 0.0) - jnp.log1p(jnp.exp(-jnp.abs(x)))


def _gelu_tanh(x):
    return 0.5 * x * (1.0 + jnp.tanh(0.7978845608028654 * (x + 0.044715 * (x * x * x))))


def _bdot(a, b):
    return jnp.dot(a.astype(BF16), b.astype(BF16), preferred_element_type=F32)


def _bdot_nt(a, b):
    return lax.dot_general(a.astype(BF16), b.astype(BF16), (((1,), (1,)), ((), ())), preferred_element_type=F32)


def _bdot_tn(a, b):
    return lax.dot_general(a.astype(BF16), b.astype(BF16), (((0,), (0,)), ((), ())), preferred_element_type=F32)


def _split2(x):
    hi = x.astype(BF16)
    return hi, (x - hi.astype(F32)).astype(BF16)


def _split3(x):
    hi = x.astype(BF16)
    r = x - hi.astype(F32)
    mid = r.astype(BF16)
    return hi, mid, (r - mid.astype(F32)).astype(BF16)


def _mod_kernel(c_ref, w_ref, b_ref, o_ref):
    o_ref[...] = _bdot(_silu(c_ref[...]), w_ref[...]) + b_ref[...]


def cond_mod(cond8, w_mod, b_mod):
    nl, _, n6 = w_mod.shape
    tn = 1024
    return pl.pallas_call(
        _mod_kernel,
        grid=(nl, n6 // tn),
        in_specs=[
            pl.BlockSpec((SUBLANES, D), lambda l, j: (0, 0)),
            pl.BlockSpec((None, D, tn), lambda l, j: (l, 0, j)),
            pl.BlockSpec((None, 1, tn), lambda l, j: (l, 0, j)),
        ],
        out_specs=pl.BlockSpec((None, SUBLANES, tn), lambda l, j: (l, 0, j)),
        out_shape=jax.ShapeDtypeStruct((nl, SUBLANES, n6), F32),
        compiler_params=_params(2),
        name="cond_mod",
    )(cond8, w_mod, b_mod.reshape(nl, 1, n6))


def _inproj_kernel(x_ref, m_ref, w_ref, o_ref, u_ref):
    @pl.when(pl.program_id(2) == 0)
    def _():
        u_ref[...] = (x_ref[...] * (1.0 + m_ref[1:2, :]) + m_ref[0:1, :]).astype(BF16)

    o_ref[...] = jnp.dot(u_ref[...], w_ref[...], preferred_element_type=F32)


def inproj(x3, mvec, w, *, tn, tm=1024):
    B, T, _ = x3.shape
    Np = w.shape[1]
    tm = min(tm, T)
    return pl.pallas_call(
        _inproj_kernel,
        grid=(B, T // tm, Np // tn),
        in_specs=[
            pl.BlockSpec((None, tm, D), lambda b, i, j: (b, i, 0)),
            pl.BlockSpec((None, SUBLANES, D), lambda b, i, j: (b, 0, 0)),
            pl.BlockSpec((D, tn), lambda b, i, j: (0, j)),
        ],
        out_specs=pl.BlockSpec((None, tm, tn), lambda b, i, j: (b, i, j)),
        out_shape=jax.ShapeDtypeStruct((B, T, Np), F32),
        scratch_shapes=[pltpu.VMEM((tm, D), BF16)],
        compiler_params=_params(3),
        name="inproj",
    )(x3, mvec, w)


def _gla_kernel(qf, kf, vf, gf, qb, kb, vb, gb, wa_ref, ba_ref, s0_ref, of_ref, ob_ref, sfin_ref, st_ref, *, tb, ns):
    s = pl.program_id(1)

    @pl.when(s == 0)
    def _():
        st_ref[...] = s0_ref[...]

    L = CHUNK
    row = lax.broadcasted_iota(jnp.int32, (L, L), 0)
    col = lax.broadcasted_iota(jnp.int32, (L, L), 1)
    row3 = lax.broadcasted_iota(jnp.int32, (L, 3 * L), 0)
    col3 = lax.rem(lax.broadcasted_iota(jnp.int32, (L, 3 * L), 1), L)
    scale = GLA_DK ** -0.5
    for d, (q_ref, k_ref, v_ref, g_ref, o_ref) in enumerate(((qf, kf, vf, gf, of_ref), (qb, kb, vb, gb, ob_ref))):
        mask = (row >= col) if d == 0 else (row <= col)
        m3 = jnp.where((row3 >= col3) if d == 0 else (row3 <= col3), 1.0, 0.0).astype(BF16)
        x_hi, x_lo = _split2(g_ref[...])
        gpre = jnp.dot(jnp.concatenate([x_hi, x_hi, x_lo], axis=1), wa_ref[d], preferred_element_type=F32)
        glog = _log_sigmoid(gpre + ba_ref[d]) * (1.0 / GLA_TAU)
        nchunk = tb // L
        for c in (range(nchunk) if d == 0 else range(nchunk - 1, -1, -1)):
            r0 = c * L
            g1, g2, g3 = _split3(glog[r0:r0 + L, :])
            b_all = jnp.dot(m3, jnp.concatenate([g1, g2, g3], axis=0), preferred_element_type=F32)
            for h in range(GLA_H):
                q = q_ref[r0:r0 + L, h * GLA_DK:(h + 1) * GLA_DK]
                k = k_ref[r0:r0 + L, h * GLA_DK:(h + 1) * GLA_DK]
                v = v_ref[r0:r0 + L, h * GLA_DV:(h + 1) * GLA_DV].astype(BF16)
                b = b_all[:, h * GLA_DK:(h + 1) * GLA_DK]
                bl = b[L - 1:L, :] if d == 0 else b[0:1, :]
                qd = (q * (jnp.exp(b) * scale)).astype(BF16)
                ki = k * jnp.exp(-b)
                ke = k * jnp.exp(bl - b)
                att = jnp.where(mask, _bdot_nt(qd, ki), 0.0)
                st = st_ref[d, h]
                o = _bdot(att, v) + _bdot_nt(qd, st)
                o_ref[r0:r0 + L, h * GLA_DV:(h + 1) * GLA_DV] = o
                st_ref[d, h] = st * jnp.exp(bl) + _bdot_tn(v, ke)

    @pl.when(s == ns - 1)
    def _():
        sfin_ref[...] = st_ref[...]


def gla_scan(P, wa, ba, s0, *, tb=512):
    B, T, _ = P.shape
    tb = min(tb, T)
    ns = T // tb
    fwd = lambda b, s: s
    bwd = lambda b, s: ns - 1 - s

    def specs(pos):
        return [
            pl.BlockSpec((None, tb, GLA_QK), lambda b, s: (b, pos(b, s), 0)),
            pl.BlockSpec((None, tb, GLA_QK), lambda b, s: (b, pos(b, s), 1)),
            pl.BlockSpec((None, tb, GLA_V), lambda b, s: (b, pos(b, s), 1)),
            pl.BlockSpec((None, tb, LANES), lambda b, s: (b, pos(b, s), EV_GATE_COL // LANES)),
        ]

    st_shape = (2, GLA_H, GLA_DV, GLA_DK)
    st_spec = pl.BlockSpec((None,) + st_shape, lambda b, s: (b, 0, 0, 0, 0))
    return pl.pallas_call(
        functools.partial(_gla_kernel, tb=tb, ns=ns),
        grid=(B, ns),
        in_specs=specs(fwd) + specs(bwd) + [
            pl.BlockSpec((2, 3 * LANES, GLA_QK), lambda b, s: (0, 0, 0)),
            pl.BlockSpec((2, 1, GLA_QK), lambda b, s: (0, 0, 0)),
            st_spec,
        ],
        out_specs=[
            pl.BlockSpec((None, tb, GLA_V), lambda b, s: (b, s, 0)),
            pl.BlockSpec((None, tb, GLA_V), lambda b, s: (b, ns - 1 - s, 0)),
            st_spec,
        ],
        out_shape=[
            jax.ShapeDtypeStruct((B, T, GLA_V), F32),
            jax.ShapeDtypeStruct((B, T, GLA_V), F32),
            jax.ShapeDtypeStruct((B,) + st_shape, F32),
        ],
        scratch_shapes=[pltpu.VMEM(st_shape, F32)],
        compiler_params=_params(2),
        name="gla_scan",
    )(P, P, P, P, P, P, P, P, wa, ba, s0)


def _lru_kernel(xf, pf, nf, xb, pb, nb_, cw_ref, cb_ref, w_ref, bias_ref, lam_ref, h0_ref,
                hf_ref, hb_ref, hfin_ref, a_s, u_s, h_s, *, tm, nt):
    i = pl.program_id(1)

    @pl.when(i == 0)
    def _():
        h_s[...] = h0_ref[...]

    cw = cw_ref[...]
    off = SUBLANES - CONV_W // 2
    for d, (x_ref, p_ref, n_ref) in enumerate(((xf, pf, nf), (xb, pb, nb_))):
        ti = i if d == 0 else nt - 1 - i
        prev = jnp.where(ti > 0, p_ref[...], 0.0)
        nxt = jnp.where(ti < nt - 1, n_ref[...], 0.0)
        ext = jnp.concatenate([prev, x_ref[...], nxt], axis=0)
        xc = cb_ref[...] + (ext[off:off + tm] * cw[0:1] + ext[off + 1:off + 1 + tm] * cw[1:2]
                            + ext[off + 2:off + 2 + tm] * cw[2:3] + ext[off + 3:off + 3 + tm] * cw[3:4])
        for n in range(LRU_NB):
            cs = slice(n * LRU_BW, (n + 1) * LRU_BW)
            xn = xc[:, cs]
            z = _bdot(xn, w_ref[d, n]) + bias_ref[d, n]
            r = _sigmoid_tanh(z[:, :LRU_BW])
            ig = _sigmoid_tanh(z[:, LRU_BW:])
            a = jnp.exp((LRU_C * r) * _log_sigmoid(lam_ref[d:d + 1, cs]))
            a_s[d, :, cs] = a
            u_s[d, :, cs] = jnp.sqrt(1.0 - a * a) * (ig * xn)

    sub = lax.broadcasted_iota(jnp.int32, (SUBLANES, LRU_W), 0)

    def group(g, carry):
        hf, hb = carry
        bf = pl.multiple_of(g * SUBLANES, SUBLANES)
        bb = pl.multiple_of(tm - SUBLANES - g * SUBLANES, SUBLANES)
        a_f, u_f = a_s[0, pl.ds(bf, SUBLANES), :], u_s[0, pl.ds(bf, SUBLANES), :]
        a_b, u_b = a_s[1, pl.ds(bb, SUBLANES), :], u_s[1, pl.ds(bb, SUBLANES), :]
        for j in range(SUBLANES):
            hf = jnp.where(sub == j, a_f * pltpu.roll(hf, 1, axis=0) + u_f, hf)
            jb = SUBLANES - 1 - j
            hb = jnp.where(sub == jb, a_b * pltpu.roll(hb, SUBLANES - 1, axis=0) + u_b, hb)
        hf_ref[pl.ds(bf, SUBLANES), :] = hf
        hb_ref[pl.ds(bb, SUBLANES), :] = hb
        return hf, hb

    hf, hb = lax.fori_loop(0, tm // SUBLANES, group, (h_s[0], h_s[1]))
    h_s[0] = hf
    h_s[1] = hb

    @pl.when(i == nt - 1)
    def _():
        hfin_ref[...] = h_s[...]


def lru_mix(P, conv_w, conv_b, wdir, bdir, lam, h0, *, tm=512):
    B, T, _ = P.shape
    tm = min(tm, T)
    nt = T // tm
    xcol = 3
    r8 = tm // SUBLANES
    n8 = T // SUBLANES
    fwd = lambda i: i
    bwd = lambda i: nt - 1 - i

    def xspecs(pos):
        return [
            pl.BlockSpec((None, tm, LRU_W), lambda b, i: (b, pos(i), xcol)),
            pl.BlockSpec((None, SUBLANES, LRU_W), lambda b, i: (b, jnp.maximum(pos(i) * r8 - 1, 0), xcol)),
            pl.BlockSpec((None, SUBLANES, LRU_W), lambda b, i: (b, jnp.minimum((pos(i) + 1) * r8, n8 - 1), xcol)),
        ]

    hspec = pl.BlockSpec((None, 2, SUBLANES, LRU_W), lambda b, i: (b, 0, 0, 0))
    seq = jax.ShapeDtypeStruct((B, T, LRU_W), F32)
    return pl.pallas_call(
        functools.partial(_lru_kernel, tm=tm, nt=nt),
        grid=(B, nt),
        in_specs=xspecs(fwd) + xspecs(bwd) + [
            _const_spec((CONV_W, LRU_W)), _const_spec((1, LRU_W)),
            _const_spec((2, LRU_NB, LRU_BW, 2 * LRU_BW)), _const_spec((2, LRU_NB, 1, 2 * LRU_BW)),
            _const_spec((2, LRU_W)), hspec,
        ],
        out_specs=[
            pl.BlockSpec((None, tm, LRU_W), lambda b, i: (b, i, 0)),
            pl.BlockSpec((None, tm, LRU_W), lambda b, i: (b, nt - 1 - i, 0)),
            hspec,
        ],
        out_shape=[seq, seq, jax.ShapeDtypeStruct((B, 2, SUBLANES, LRU_W), F32)],
        scratch_shapes=[pltpu.VMEM((2, tm, LRU_W), F32), pltpu.VMEM((2, tm, LRU_W), F32),
                        pltpu.VMEM((2, SUBLANES, LRU_W), F32)],
        compiler_params=_params(2),
        name="lru_mix",
    )(P, P, P, P, P, P, conv_w, conv_b.reshape(1, LRU_W), wdir, bdir, lam, h0)


ML_AUG = ML_DV + LANES
ML_L = 256


def _chunk_rows(ref, w):
    L, nwl, width = ref.shape
    flat = ref.reshape(L * nwl, width)
    return flat.at[pl.ds(w, L, stride=nwl), :] if nwl > 1 else flat


def _mlstm_kernel(qf, kf, vf0, vf1, gf, qb, kb, vb0, vb1, gb, bg_ref, c0_ref, m0_ref,
                  hf0_ref, hf1_ref, hb0_ref, hb1_ref, cfin_ref, mfin_ref, c_ref, m_ref, *, L, nwl, ns):
    s = pl.program_id(2)

    @pl.when(s == 0)
    def _():
        c_ref[...] = c0_ref[...]
        m_ref[...] = m0_ref[...]

    row = lax.broadcasted_iota(jnp.int32, (L, L), 0)
    col = lax.broadcasted_iota(jnp.int32, (L, L), 1)
    nr = nwl * SUBLANES
    sub = lax.rem(lax.broadcasted_iota(jnp.int32, (nr, L), 0), SUBLANES)
    lane = lax.broadcasted_iota(jnp.int32, (nr, L), 1)
    is_f = (sub == 1) | (sub == 3)
    ones_col = (lax.broadcasted_iota(jnp.int32, (L, LANES), 1) == 0).astype(BF16)
    kscale = ML_DK ** -0.5
    bias = jnp.concatenate([bg_ref[...]] * nwl, axis=0)

    chunk_rows = _chunk_rows

    def gate_vectors(g_ref, d):
        pre = g_ref[...].reshape(nr, L) + bias
        xg = jnp.where(is_f, _log_sigmoid(pre), pre)
        cs = xg
        k_ = 1
        while k_ < L:
            if d == 0:
                cs = cs + jnp.where(lane >= k_, pltpu.roll(cs, k_, axis=1), 0.0)
            else:
                cs = cs + jnp.where(lane < L - k_, pltpu.roll(cs, L - k_, axis=1), 0.0)
            k_ *= 2
        ri, rf = 2 * d, 2 * d + 1
        pad = jnp.zeros((LANES - nr, L), F32)
        cs_t = jnp.concatenate([cs, pad], axis=0).T
        xg_t = jnp.concatenate([xg, pad], axis=0).T
        out = []
        for w in range(nwl):
            jb, ji = SUBLANES * w + rf, SUBLANES * w + ri
            out.append((cs[jb:jb + 1, :], xg[ji:ji + 1, :], cs_t[:, jb:jb + 1], xg_t[:, ji:ji + 1]))
        return out

    vec = (gate_vectors(gf, 0), gate_vectors(gb, 1))
    state = [(m_ref[d][0:1, 0:1], c_ref[d]) for d in range(2)]
    work = [(0, w, qf, kf, (vf0, vf1), (hf0_ref, hf1_ref)) for w in range(nwl)]
    work_b = [(1, w, qb, kb, (vb0, vb1), (hb0_ref, hb1_ref)) for w in range(nwl - 1, -1, -1)]
    for pair in zip(work, work_b):
      for d, w, q_ref, k_ref, v_refs, h_refs in pair:
        mask = (row >= col) if d == 0 else (row <= col)
        b_row, i_row, b_col, i_col = vec[d][w]
        b_last = b_row[:, L - 1:L] if d == 0 else b_row[:, 0:1]
        dmat = jnp.where(mask, b_col + (i_row - b_row), -jnp.inf)
        m_loc = jnp.max(dmat, axis=1, keepdims=True)
        q = chunk_rows(q_ref, w)[...].astype(BF16)
        k = chunk_rows(k_ref, w)[...] * kscale
        vaug = jnp.concatenate([chunk_rows(v_refs[0], w)[...].astype(BF16), chunk_rows(v_refs[1], w)[...].astype(BF16),
                                ones_col], axis=1)
        sc = _bdot_nt(q, k) * jnp.exp(dmat - m_loc)
        n_loc = _bdot(sc, vaug)
        g_row = b_last - b_row + i_row
        m_k = jnp.max(g_row, axis=1, keepdims=True)
        kv_loc = _bdot_tn(jnp.exp(b_last - b_col + i_col - m_k) * k, vaug)
        m, caug = state[d]
        inter = b_col + m
        m_t = jnp.maximum(inter, m_loc)
        acc = jnp.exp(m_loc - m_t) * n_loc + jnp.exp(inter - m_t) * _bdot(q, caug)
        den = acc[:, ML_DV:ML_DV + 1]
        h = acc[:, :ML_DV] / jnp.maximum(jnp.abs(den), jnp.exp(-m_t))
        chunk_rows(h_refs[0], w)[...] = h[:, :LANES]
        chunk_rows(h_refs[1], w)[...] = h[:, LANES:]
        m_new = jnp.maximum(b_last + m, m_k)
        state[d] = (m_new, jnp.exp(b_last + m - m_new) * caug + jnp.exp(m_k - m_new) * kv_loc)

    for d in range(2):
        m_ref[d] = jnp.broadcast_to(state[d][0], (SUBLANES, LANES))
        c_ref[d] = state[d][1]

    @pl.when(s == ns - 1)
    def _():
        cfin_ref[...] = c_ref[...]
        mfin_ref[...] = m_ref[...]


def mlstm_scan(P, gt, bgate, c0, m0, *, n_col):
    B, T, Np = P.shape
    L = T // n_col
    nwl = min(n_col, SUBLANES)
    ns = n_col // nwl
    P5 = P.reshape(B, L, ns, nwl, Np)
    fwd = lambda s: s
    bwd = lambda s: ns - 1 - s

    def specs(pos):
        return [
            pl.BlockSpec((None, L, None, nwl, ML_DK), lambda b, h, s: (b, 0, pos(s), 0, h)),
            pl.BlockSpec((None, L, None, nwl, ML_DK), lambda b, h, s: (b, 0, pos(s), 0, ML_H + h)),
            pl.BlockSpec((None, L, None, nwl, LANES), lambda b, h, s: (b, 0, pos(s), 0, 2 * ML_H + 2 * h)),
            pl.BlockSpec((None, L, None, nwl, LANES), lambda b, h, s: (b, 0, pos(s), 0, 2 * ML_H + 2 * h + 1)),
            pl.BlockSpec((None, None, nwl, SUBLANES, L), lambda b, h, s: (b, h, pos(s), 0, 0)),
        ]

    cshape = (2, ML_DK, ML_AUG)
    mshape = (2, SUBLANES, LANES)
    cspec = pl.BlockSpec((None, 2, None, ML_DK, ML_AUG), lambda b, h, s: (b, 0, h, 0, 0))
    mspec = pl.BlockSpec((None, 2, None, SUBLANES, LANES), lambda b, h, s: (b, 0, h, 0, 0))
    seq = jax.ShapeDtypeStruct((B, L, ns, nwl, ML_H * LANES), F32)
    ospec = lambda pos: pl.BlockSpec((None, L, None, nwl, LANES), lambda b, h, s: (b, 0, pos(s), 0, h))
    hf0, hf1, hb0, hb1, cfin, mfin = pl.pallas_call(
        functools.partial(_mlstm_kernel, L=L, nwl=nwl, ns=ns),
        grid=(B, ML_H, ns),
        in_specs=specs(fwd) + specs(bwd) + [
            pl.BlockSpec((None, SUBLANES, 1), lambda b, h, s: (h, 0, 0)),
            cspec, mspec,
        ],
        out_specs=[ospec(fwd), ospec(fwd), ospec(bwd), ospec(bwd), cspec, mspec],
        out_shape=[
            seq, seq, seq, seq,
            jax.ShapeDtypeStruct((B, 2, ML_H, ML_DK, ML_AUG), F32),
            jax.ShapeDtypeStruct((B, 2, ML_H, SUBLANES, LANES), F32),
        ],
        scratch_shapes=[pltpu.VMEM(cshape, F32), pltpu.VMEM(mshape, F32)],
        compiler_params=_params(3),
        name="mlstm_scan",
    )(P5, P5, P5, P5, gt, P5, P5, P5, P5, gt, bgate, c0, m0)
    flat = lambda a: a.reshape(B, T, ML_H * LANES)
    return (flat(hf0), flat(hf1), flat(hb0), flat(hb1)), cfin, mfin


def _layer_norm(z, g, b):
    mu = jnp.mean(z, axis=-1, keepdims=True)
    zc = z - mu
    var = jnp.mean(zc * zc, axis=-1, keepdims=True)
    return zc * lax.rsqrt(var + EPS) * g + b


def _head_rms(o, g, n_heads, width):
    parts = []
    for h in range(n_heads):
        oh = o[:, h * width:(h + 1) * width]
        parts.append(oh * lax.rsqrt(jnp.mean(oh * oh, axis=-1, keepdims=True) + EPS) * g)
    return jnp.concatenate(parts, axis=-1)


def _post(y, x_ref, m_ref, lng_ref, lnb_ref, rw_refs, x1_ref, xm_ref, eid_ref, gate_ref):
    w_r, bias = rw_refs
    x1 = _layer_norm(ALPHA * x_ref[...] + m_ref[2:3, :] * y, lng_ref[...], lnb_ref[...])
    x1_ref[...] = x1
    xm = x1 * (1.0 + m_ref[4:5, :]) + m_ref[3:4, :]
    tm = xm.shape[0]
    for j in range(TOK_ROWS):
        xm_ref[pl.ds(j, tm, stride=TOK_ROWS), :] = xm[:, j * LANES:(j + 1) * LANES]
    lg = jnp.dot(xm.astype(BF16), w_r[...], preferred_element_type=F32) + bias[...]
    lane = lax.broadcasted_iota(jnp.int32, lg.shape, 1)
    neg = -jnp.inf
    glm = jnp.where(lane < N_GROUPS, lg, neg)
    gmax = jnp.max(glm, axis=-1, keepdims=True)
    gidx = jnp.min(jnp.where(glm == gmax, lane, LANES), axis=-1, keepdims=True)
    p_g = 1.0 / jnp.sum(jnp.exp(glm - gmax), axis=-1, keepdims=True)
    e_lane = lane - N_GROUPS
    in_group = (lax.shift_right_arithmetic(e_lane, 3) == gidx) & (e_lane < N_EXP)
    elm = jnp.where(in_group, lg, neg)
    m1 = jnp.max(elm, axis=-1, keepdims=True)
    i1 = jnp.min(jnp.where(elm == m1, e_lane, LANES), axis=-1, keepdims=True)
    elm2 = jnp.where(e_lane == i1, neg, elm)
    m2 = jnp.max(elm2, axis=-1, keepdims=True)
    i2 = jnp.min(jnp.where(elm2 == m2, e_lane, LANES), axis=-1, keepdims=True)
    e2 = jnp.exp(m2 - m1)
    g1 = p_g / (1.0 + e2)
    g2 = g1 * e2
    eid_ref[...] = jnp.where(lane == 0, i1, jnp.where(lane == 1, i2, 0)).T[:SUBLANES, :]
    gate_ref[...] = jnp.where(lane == 0, g1, jnp.where(lane == 1, g2, 0.0)).T[:SUBLANES, :]


def _out_even_kernel(of, ob, r, hf, hb, xg, x, m, gg, wout, lng, lnb, w_r, bias, *rest):
    x1_ref, xm_ref, eid_ref, gate_ref = rest[-4:]
    rg = r[...]
    o = _head_rms(of[...] + ob[...], gg[...], GLA_H, GLA_DV) * (rg * _sigmoid_tanh(rg))
    y2 = (hf[...] + hb[...]) * _gelu_tanh(xg[...])
    mix = jnp.concatenate([o.astype(BF16), y2.astype(BF16)], axis=-1)
    y = jnp.dot(mix, wout[...], preferred_element_type=F32)
    _post(y, x, m, lng, lnb, (w_r, bias), x1_ref, xm_ref, eid_ref, gate_ref)


def _out_odd_kernel(hf0, hf1, hb0, hb1, og, x, m, gg, wout, lng, lnb, w_r, bias, *rest):
    x1_ref, xm_ref, eid_ref, gate_ref = rest[-4:]
    lo = hf0[...] + hb0[...]
    hi = hf1[...] + hb1[...]
    hsum = jnp.concatenate([half[:, h * LANES:(h + 1) * LANES] for h in range(ML_H) for half in (lo, hi)], axis=-1)
    o = _head_rms(hsum, gg[...], ML_H, ML_DV) * _sigmoid_tanh(og[...])
    y = jnp.dot(o.astype(BF16), wout[...], preferred_element_type=F32)
    _post(y, x, m, lng, lnb, (w_r, bias), x1_ref, xm_ref, eid_ref, gate_ref)


def _router_weights(w_group, b_group, w_expert, b_expert):
    n = N_GROUPS + N_EXP
    wp = jnp.zeros((D, LANES), F32).at[:, :N_GROUPS].set(w_group).at[:, N_GROUPS:n].set(w_expert)
    bias = jnp.zeros((1, LANES), F32).at[0, :N_GROUPS].set(b_group).at[0, N_GROUPS:n].set(b_expert)
    return wp.astype(BF16), bias


def _const_spec(shape):
    nd = len(shape)
    return pl.BlockSpec(shape, lambda *_: (0,) * nd)


def out_block(kind, acts, x, mvec, gg, wout, lng, lnb, rw, *, n_tok_all, row_off, carry=None, tm=256):
    B, T, _ = x.shape
    tm = min(tm, T)
    nt = T // tm
    tok = lambda width: pl.BlockSpec((None, tm, width), lambda b, i: (b, i, 0))
    act = lambda width, cb=0: pl.BlockSpec((None, tm, width), lambda b, i: (b, i, cb))
    if kind == "even":
        of, ob, P, hf, hb = acts
        a_in = [of, ob, P, hf, hb, P]
        a_specs = [act(GLA_V), act(GLA_V), act(GLA_V, 2), act(LRU_W), act(LRU_W), act(LRU_W, 4)]
        body = _out_even_kernel
    else:
        halves, P = acts
        a_in = list(halves) + [P]
        a_specs = [act(ML_H * LANES)] * 4 + [act(ML_V, 2)]
        body = _out_odd_kernel
    w_in = [mvec, gg.reshape(1, -1), wout, lng.reshape(1, D), lnb.reshape(1, D)] + list(rw)
    w_specs = [pl.BlockSpec((None, SUBLANES, D), lambda b, i: (b, 0, 0))] + [_const_spec(a.shape) for a in w_in[1:]]
    rb = row_off // tm
    blk_idx = lambda b, i: rb + b * nt + i
    route = pl.BlockSpec((SUBLANES, tm), lambda b, i: (0, blk_idx(b, i)))
    out_specs = [tok(D), pl.BlockSpec((tm * TOK_ROWS, LANES), lambda b, i: (blk_idx(b, i), 0)), route, route]
    out_shape = [jax.ShapeDtypeStruct(x.shape, F32), jax.ShapeDtypeStruct((n_tok_all * TOK_ROWS, LANES), F32),
                 jax.ShapeDtypeStruct((SUBLANES, n_tok_all), jnp.int32), jax.ShapeDtypeStruct((SUBLANES, n_tok_all), F32)]
    c_in, c_specs, aliases = [], [], {}
    if carry is None and n_tok_all != B * T:
        carry = tuple(jnp.zeros(s.shape, s.dtype) for s in out_shape[1:])
    if carry is not None:
        c_in = list(carry)
        c_specs = [pl.BlockSpec(memory_space=pl.ANY)] * 3
        base = len(a_in) + 1 + len(w_in)
        aliases = {base: 1, base + 1: 2, base + 2: 3}
    x1, xm, eid, gate = pl.pallas_call(
        body,
        grid=(B, nt),
        in_specs=a_specs + [tok(D)] + w_specs + c_specs,
        out_specs=out_specs,
        out_shape=out_shape,
        input_output_aliases=aliases,
        compiler_params=_params(2),
        name="out_" + kind,
    )(*a_in, x, *w_in, *c_in)
    return x1, xm, eid, gate


def _expert_kernel(be_ref, src_c, src_n, dst_p, dst_c, g_ref, wg, wu, wd, xm_hbm, y_hbm, xbuf, ybuf, gs, ss,
                   *, nb, n_slots):
    del be_ref
    n = pl.program_id(0)
    blk, tr, tp = MOE_BLK, TOK_ROWS, TOK_PITCH

    def gather(idx_ref, slot):
        for r in range(blk):
            row = pl.multiple_of(idx_ref[0, r], tr)
            pltpu.make_async_copy(xm_hbm.at[pl.ds(row, tr)], xbuf.at[slot, pl.ds(r * tp, tr)], gs.at[slot]).start()

    def scatter(idx_ref, slot):
        for r in range(blk):
            row = pl.multiple_of(idx_ref[0, r], tr)
            pltpu.make_async_copy(ybuf.at[slot, pl.ds(r * tp, tr)], y_hbm.at[pl.ds(row, tr)], ss.at[slot]).start()

    def wait_gather(slot):
        pltpu.make_async_copy(xm_hbm.at[pl.ds(0, blk * tr)], xbuf.at[slot, pl.ds(0, blk * tr)], gs.at[slot]).wait()

    def wait_scatter(slot):
        pltpu.make_async_copy(ybuf.at[slot, pl.ds(0, blk * tr)], y_hbm.at[pl.ds(0, blk * tr)], ss.at[slot]).wait()

    def mlp(slot):
        xs, ys = xbuf.at[slot], ybuf.at[slot]
        xb = jnp.concatenate([xs[pl.ds(j, blk, stride=tp), :] for j in range(tr)], axis=1).astype(BF16)
        hg = jnp.dot(xb, wg[...].astype(BF16), preferred_element_type=F32)
        hu = jnp.dot(xb, wu[...].astype(BF16), preferred_element_type=F32)
        h = (hg * _sigmoid_tanh(hg) * hu).astype(BF16)
        y = jnp.dot(h, wd[...].astype(BF16), preferred_element_type=F32)
        g_col = jnp.broadcast_to(g_ref[...], (LANES, blk)).T
        for j in range(tr):
            ys[pl.ds(j, blk, stride=tp), :] = y[:, j * LANES:(j + 1) * LANES] * g_col

    s = lax.rem(n, 2)

    @pl.when(n == 0)
    def _():
        ybuf[...] = jnp.zeros_like(ybuf)
        pltpu.make_async_copy(ybuf.at[0, pl.ds(0, blk * tr)], y_hbm.at[pl.ds((n_slots + blk) * tr, blk * tr)],
                              ss.at[0]).start()
        gather(src_c, 0)

    wait_gather(s)
    wait_scatter(s)
    gather(src_n, 1 - s)
    scatter(dst_p, 1 - s)
    mlp(s)

    @pl.when(n == nb - 1)
    def _():
        scatter(dst_c, s)
        wait_gather(1 - s)
        wait_scatter(1 - s)
        wait_scatter(s)


def moe_experts(xm, block_e, slot_src, slot_dst, slot_gate, wg, wu, wd, layer):
    nb, blk = slot_src.shape
    n_slots = nb * blk
    src3 = (slot_src * TOK_ROWS).reshape(nb, 1, blk)
    spare = (n_slots + jnp.arange(blk, dtype=jnp.int32)).reshape(1, 1, blk)
    dst3 = jnp.concatenate([spare, slot_dst.reshape(nb, 1, blk)], axis=0) * TOK_ROWS
    gate3 = slot_gate.reshape(nb, 1, blk)
    smem_blk = lambda f: pl.BlockSpec((None, 1, blk), f, memory_space=pltpu.SMEM)
    grid_spec = pltpu.PrefetchScalarGridSpec(
        num_scalar_prefetch=1,
        grid=(nb,),
        in_specs=[
            smem_blk(lambda n, be: (n, 0, 0)),
            smem_blk(lambda n, be: (jnp.minimum(n + 1, nb - 1), 0, 0)),
            smem_blk(lambda n, be: (n, 0, 0)),
            smem_blk(lambda n, be: (n + 1, 0, 0)),
            pl.BlockSpec((None, 1, blk), lambda n, be: (n, 0, 0)),
            pl.BlockSpec((None, None, D, D_EXP), lambda n, be: (layer, be[n], 0, 0)),
            pl.BlockSpec((None, None, D, D_EXP), lambda n, be: (layer, be[n], 0, 0)),
            pl.BlockSpec((None, None, D_EXP, D), lambda n, be: (layer, be[n], 0, 0)),
            pl.BlockSpec(memory_space=pl.ANY),
        ],
        out_specs=pl.BlockSpec(memory_space=pl.ANY),
        scratch_shapes=[
            pltpu.VMEM((2, blk * TOK_PITCH, LANES), F32),
            pltpu.VMEM((2, blk * TOK_PITCH, LANES), F32),
            pltpu.SemaphoreType.DMA((2,)),
            pltpu.SemaphoreType.DMA((2,)),
        ],
    )
    return pl.pallas_call(
        functools.partial(_expert_kernel, nb=nb, n_slots=n_slots),
        grid_spec=grid_spec,
        out_shape=jax.ShapeDtypeStruct(((n_slots + 2 * blk) * TOK_ROWS, LANES), F32),
        compiler_params=_params(1),
        name="moe_experts",
    )(block_e, src3, src3, dst3, dst3, gate3, wg, wu, wd, xm)


def moe_tables(eid, gate, n_tok):
    blk = MOE_BLK
    n_assign = 2 * n_tok
    flat_e = eid[:2].reshape(-1)
    flat_g = gate[:2].reshape(-1)
    counts = jnp.sum((flat_e[:, None] == jnp.arange(N_EXP, dtype=jnp.int32)[None, :]).astype(jnp.int32), axis=0)
    padded = (counts + blk - 1) // blk * blk
    pad_end = jnp.cumsum(padded)
    pad_start = pad_end - padded
    cnt_start = jnp.cumsum(counts) - counts
    order = jnp.argsort(flat_e).astype(jnp.int32)
    nb = -(-(n_assign + N_EXP * (blk - 1)) // blk)
    bstart = jnp.arange(nb, dtype=jnp.int32) * blk
    block_e = jnp.minimum(jnp.sum((pad_end[None, :] <= bstart[:, None]).astype(jnp.int32), axis=1), N_EXP - 1)
    lane = jnp.arange(blk, dtype=jnp.int32)[None, :]
    j = (bstart - pad_start[block_e])[:, None] + lane
    cnt_b = counts[block_e][:, None]
    valid = j < cnt_b
    n_valid_before = cnt_start[block_e][:, None] + jnp.minimum(j, cnt_b)
    a = order[jnp.clip(n_valid_before, 0, n_assign - 1)]
    slot = bstart[:, None] + lane
    slot_src = jnp.where(valid, jnp.where(a >= n_tok, a - n_tok, a), 0).astype(jnp.int32)
    slot_dst = jnp.where(valid, a, n_assign + slot - n_valid_before).astype(jnp.int32)
    slot_gate = jnp.where(valid, flat_g[a], 0.0)
    return block_e.astype(jnp.int32), slot_src, slot_dst, slot_gate


def _combine_kernel(x_ref, y0_ref, y1_ref, m_ref, g_ref, b_ref, o_ref):
    tm = x_ref.shape[0]
    y = jnp.concatenate([y0_ref[pl.ds(j, tm, stride=TOK_ROWS), :] + y1_ref[pl.ds(j, tm, stride=TOK_ROWS), :]
                         for j in range(TOK_ROWS)], axis=1)
    o_ref[...] = _layer_norm(ALPHA * x_ref[...] + m_ref[5:6, :] * y, g_ref[...], b_ref[...])


def moe_combine(x1, Y, mvec, lng, lnb, *, n_tok, row_off, tm=512):
    B, T, _ = x1.shape
    tm = min(tm, T)
    nt = T // tm
    rb = row_off // tm
    kb = n_tok // tm
    return pl.pallas_call(
        _combine_kernel,
        grid=(B, nt),
        in_specs=[
            pl.BlockSpec((None, tm, D), lambda b, i: (b, i, 0)),
            pl.BlockSpec((tm * TOK_ROWS, LANES), lambda b, i: (rb + b * nt + i, 0)),
            pl.BlockSpec((tm * TOK_ROWS, LANES), lambda b, i: (kb + rb + b * nt + i, 0)),
            pl.BlockSpec((None, SUBLANES, D), lambda b, i: (b, 0, 0)),
            _const_spec((1, D)), _const_spec((1, D)),
        ],
        out_specs=pl.BlockSpec((None, tm, D), lambda b, i: (b, i, 0)),
        out_shape=jax.ShapeDtypeStruct((B, T, D), F32),
        compiler_params=_params(2),
        name="moe_combine",
    )(x1, Y, Y, mvec, lng.reshape(1, D), lnb.reshape(1, D))


def _even_w_in(w):
    g0 = GLA_QK * 2 + GLA_V * 2
    g1 = g0 + 2 * GLA_RANK
    pad = jnp.zeros((D, EV_NP - w.shape[1]), w.dtype)
    return jnp.concatenate([w[:, :g0], w[:, g1:], w[:, g0:g1], pad], axis=1).astype(BF16)


def _odd_w_in(w):
    return jnp.concatenate([w, jnp.zeros((D, OD_NP - w.shape[1]), w.dtype)], axis=1).astype(BF16)


def _lru_weights(w_r, b_r, w_i, b_i):
    wdir = jnp.concatenate([w_r, w_i], axis=-1).astype(BF16)
    blk = lambda b: b.reshape(2, LRU_NB, 1, LRU_BW)
    bdir = jnp.concatenate([blk(b_r), blk(b_i)], axis=-1)
    return wdir, bdir


def _gla_gate_weights(w_a2, b_a):
    wa = jnp.zeros((2, LANES, GLA_QK), F32)
    wa = wa.at[0, :GLA_RANK].set(w_a2[0]).at[1, GLA_RANK:2 * GLA_RANK].set(w_a2[1])
    w_hi, w_lo = _split2(wa)
    return jnp.concatenate([w_hi, w_lo, w_hi], axis=1), b_a.reshape(2, 1, GLA_QK)


def _mlstm_gate_table(P, n_col):
    B, T, _ = P.shape
    g = P[:, :, OD_GATE_COL:OD_GATE_COL + 4 * ML_H].reshape(B, T // n_col, n_col, 4, ML_H)
    g = jnp.transpose(g, (0, 4, 2, 3, 1))
    return jnp.concatenate([g, jnp.zeros_like(g)], axis=3)


def _mvec(mods_l, rows):
    m6 = jnp.stack([mods_l[r] for r in rows]).reshape(len(rows), 6, D)
    return jnp.concatenate([m6, jnp.zeros((len(rows), 2, D), F32)], axis=1)


def kernel(x, c, ctx, c_ctx, w_mod, b_mod, ln_g, ln_b, ev_w_in, gla_w_a2, gla_b_a, gla_norm, lru_conv_w, lru_conv_b, lru_w_r, lru_b_r, lru_w_i, lru_b_i, lru_lam, ev_w_out, od_w_in, mlstm_b_gate, mlstm_norm, od_w_out, moe_w_group, moe_b_group, moe_w_expert, moe_b_expert, moe_w_gate, moe_w_up, moe_w_down):
    B, S, _ = x.shape
    T_ctx = ctx.shape[1]
    cond8 = jnp.zeros((SUBLANES, D), F32).at[:B].set(c).at[B].set(c_ctx)
    mods = cond_mod(cond8, w_mod, b_mod)
    h_ctx = ctx
    for l in range(DEPTH):
        j = l // 2
        last = l == DEPTH - 1
        m_lat = _mvec(mods[l], list(range(B)))
        m_ctx = _mvec(mods[l], [B] * B)
        rw = _router_weights(moe_w_group[l], moe_b_group[l], moe_w_expert[l], moe_b_expert[l])
        n_tok = B * S if last else B * (S + T_ctx)
        if l % 2 == 0:
            w_in = _even_w_in(ev_w_in[j])
            wa, ba = _gla_gate_weights(gla_w_a2[j], gla_b_a[j])
            wdir, bdir = _lru_weights(lru_w_r[j], lru_b_r[j], lru_w_i[j], lru_b_i[j])
            w_out = ev_w_out[j].astype(BF16)

            def mix_even(xin, mv, s0, h0):
                P = inproj(xin, mv, w_in, tn=EV_TN)
                of, ob, sfin = gla_scan(P, wa, ba, s0)
                hf, hb, hfin = lru_mix(P, lru_conv_w[j], lru_conv_b[j], wdir, bdir, lru_lam[j], h0)
                return (of, ob, P, hf, hb), sfin, hfin

            s0 = jnp.zeros((B, 2, GLA_H, GLA_DV, GLA_DK), F32)
            h0 = jnp.zeros((B, 2, SUBLANES, LRU_W), F32)
            acts_c, s_c, h_c = mix_even(h_ctx, m_ctx, s0, h0)
            acts_l, _, _ = mix_even(x, m_lat, s_c, h_c)
            carry = None
            if not last:
                hc1, *carry = out_block("even", acts_c, h_ctx, m_ctx, gla_norm[j], w_out, ln_g[l, 0], ln_b[l, 0], rw,
                                        n_tok_all=n_tok, row_off=B * S)
            x1, xm, eid, gate = out_block("even", acts_l, x, m_lat, gla_norm[j], w_out, ln_g[l, 0], ln_b[l, 0], rw,
                                          n_tok_all=n_tok, row_off=0, carry=carry)
        else:
            w_in = _odd_w_in(od_w_in[j])
            w_out = od_w_out[j].astype(BF16)
            bgate = jnp.concatenate([mlstm_b_gate[j].T, jnp.zeros((ML_H, 4), F32)], axis=1).reshape(ML_H, SUBLANES, 1)

            def mix_odd(xin, mv, c0, m0, n_col):
                P = inproj(xin, mv, w_in, tn=OD_TN)
                gt = _mlstm_gate_table(P, n_col)
                halves, cfin, mfin = mlstm_scan(P, gt, bgate, c0, m0, n_col=n_col)
                return (halves, P), cfin, mfin

            c0 = jnp.zeros((B, 2, ML_H, ML_DK, ML_AUG), F32)
            m0 = jnp.full((B, 2, ML_H, SUBLANES, LANES), M_INIT, F32)
            acts_c, c_c, m_c = mix_odd(h_ctx, m_ctx, c0, m0, 1)
            acts_l, _, _ = mix_odd(x, m_lat, c_c, m_c, GRID_W)
            carry = None
            if not last:
                hc1, *carry = out_block("odd", acts_c, h_ctx, m_ctx, mlstm_norm[j], w_out, ln_g[l, 0], ln_b[l, 0], rw,
                                        n_tok_all=n_tok, row_off=B * S)
            x1, xm, eid, gate = out_block("odd", acts_l, x, m_lat, mlstm_norm[j], w_out, ln_g[l, 0], ln_b[l, 0], rw,
                                          n_tok_all=n_tok, row_off=0, carry=carry)
        block_e, slot_src, slot_dst, slot_gate = moe_tables(eid, gate, n_tok)
        Y = moe_experts(xm, block_e, slot_src, slot_dst, slot_gate, moe_w_gate, moe_w_up, moe_w_down, l)
        x = moe_combine(x1, Y, m_lat, ln_g[l, 1], ln_b[l, 1], n_tok=n_tok, row_off=0)
        if not last:
            h_ctx = moe_combine(hc1, Y, m_ctx, ln_g[l, 1], ln_b[l, 1], n_tok=n_tok, row_off=B * S)
    return x
```
